```python
import jax, jax.numpy as jnp
from jax import lax
import numpy as np

D_MODEL = 1024
BATCH = 8
SEQ = 8192
DEPTH = 1

GRID_W = 64
CTX_LEN = 256
EPS = 1e-6
M_HEADS = 4
M_HEAD_DIM = D_MODEL // M_HEADS
M_WIDTH = M_HEADS * M_HEAD_DIM
M_CHUNK = 128
CONV_K = 5
A_HEADS = 8
A_KV_HEADS = 2
A_HEAD_DIM = D_MODEL // A_HEADS
A_GROUPS = A_HEADS // A_KV_HEADS
Q_BLOCK = 128
ROPE_THETA = 10000.0
ROPE_PAIRS = A_HEAD_DIM // 4
N_BRANCH = 2
D_FF = -(-8 * D_MODEL // (3 * 256)) * 256
IN_SPLITS = (M_WIDTH, M_WIDTH, M_WIDTH, M_WIDTH, 4 * M_HEADS, A_HEADS * A_HEAD_DIM, A_KV_HEADS * A_HEAD_DIM, A_KV_HEADS * A_HEAD_DIM, N_BRANCH * D_MODEL)
D_IN = sum(IN_SPLITS)

kernel_name = 'hybrid_mlstm_gqa_prefix_block'


def rmsnorm(x, g):
    xf = x.astype(jnp.float32)
    y = xf * lax.rsqrt(jnp.mean(xf * xf, axis=-1, keepdims=True) + EPS)
    return (y * g.astype(jnp.float32)).astype(x.dtype)


def split_cols(z):
    idx = np.cumsum(IN_SPLITS)[:-1].tolist()
    return jnp.split(z, idx, axis=-1)


def short_conv(u, w, b):
    pad = CONV_K // 2
    y = lax.conv_general_dilated(u, w[:, None, :].astype(u.dtype), window_strides=(1,), padding=[(pad, pad)], dimension_numbers=('NWC', 'WIO', 'NWC'), feature_group_count=u.shape[-1])
    return y + b.astype(u.dtype)


def axial_rope_tables(rows, dtype):
    row = jnp.repeat(jnp.arange(rows, dtype=jnp.float32), GRID_W)
    col = jnp.tile(jnp.arange(GRID_W, dtype=jnp.float32), rows)
    inv = ROPE_THETA ** (-jnp.arange(ROPE_PAIRS, dtype=jnp.float32) / ROPE_PAIRS)
    ang = jnp.concatenate([row[:, None] * inv, col[:, None] * inv], axis=-1)
    return jnp.cos(ang).astype(dtype)[:, None, :], jnp.sin(ang).astype(dtype)[:, None, :]


def apply_rope(x, cos, sin):
    x1, x2 = x[..., 0::2], x[..., 1::2]
    return jnp.stack([x1 * cos - x2 * sin, x1 * sin + x2 * cos], axis=-1).reshape(x.shape)


def mlstm_scan(q, k, v, log_i, log_f, state):
    B, T, H, _ = q.shape
    nc = T // M_CHUNK

    def to_chunks(a):
        a = a.reshape((B, nc, M_CHUNK) + a.shape[2:])
        return jnp.moveaxis(a, (1, 3), (0, 2))

    tri = jnp.tril(jnp.ones((M_CHUNK, M_CHUNK), dtype=bool))

    def step(carry, blk):
        C, n, m = carry
        qc, kc, vc, li, lf = blk
        F = jnp.cumsum(lf, axis=-1)
        g = F + m[..., None]
        Dm = jnp.where(tri, F[..., :, None] - F[..., None, :] + li[..., None, :], -jnp.inf)
        mt = jnp.maximum(g, jnp.max(Dm, axis=-1))
        w_inter = jnp.exp(g - mt)
        s = jnp.einsum('bhtd,bhsd->bhts', qc, kc) * jnp.exp(Dm - mt[..., None])
        num = w_inter[..., None] * jnp.einsum('bhtd,bhde->bhte', qc, C) + jnp.einsum('bhts,bhse->bhte', s, vc)
        den = w_inter * jnp.einsum('bhtd,bhd->bht', qc, n) + jnp.sum(s, axis=-1)
        h = num / jnp.maximum(jnp.abs(den), jnp.exp(-mt))[..., None]
        FL = F[..., -1]
        dec = FL[..., None] - F + li
        m_new = jnp.maximum(FL + m, jnp.max(dec, axis=-1))
        a_prev = jnp.exp(FL + m - m_new)
        w_s = jnp.exp(dec - m_new[..., None])
        C_new = a_prev[..., None, None] * C + jnp.einsum('bhs,bhsd,bhse->bhde', w_s, kc, vc)
        n_new = a_prev[..., None] * n + jnp.einsum('bhs,bhsd->bhd', w_s, kc)
        return (C_new, n_new, m_new), h

    blocks = (to_chunks(q), to_chunks(k), to_chunks(v), to_chunks(log_i), to_chunks(log_f))
    final, h = lax.scan(step, state, blocks)
    h = jnp.moveaxis(h, (0, 2), (1, 3)).reshape(B, T, H, -1)
    return final, h


def mlstm_prep(qm, km, vm, zg, conv_w, conv_b, gate_b):
    B, T, _ = qm.shape
    qk = jax.nn.silu(short_conv(jnp.concatenate([qm, km], axis=-1), conv_w, conv_b))
    q, k = jnp.split(qk, 2, axis=-1)
    heads = lambda a: a.reshape(B, T, M_HEADS, M_HEAD_DIM).astype(jnp.float32)
    g = (zg + gate_b).astype(jnp.float32).reshape(B, T, 4, M_HEADS)
    return heads(q), heads(k) * (M_HEAD_DIM ** -0.5), heads(vm), g


def mlstm_two_way(lat, ctx):
    B = lat[0].shape[0]
    init = (jnp.zeros((B, M_HEADS, M_HEAD_DIM, M_HEAD_DIM), jnp.float32), jnp.zeros((B, M_HEADS, M_HEAD_DIM), jnp.float32), jnp.full((B, M_HEADS), -jnp.inf, jnp.float32))

    def run(stream, d, state, reverse):
        q, k, v, g = stream
        li = g[:, :, 2 * d]
        lf = jax.nn.log_sigmoid(g[:, :, 2 * d + 1])
        if reverse:
            q, k, v, li, lf = (jnp.flip(a, axis=1) for a in (q, k, v, li, lf))
        st, h = mlstm_scan(q, k, v, li, lf, state)
        if reverse:
            h = jnp.flip(h, axis=1)
        return st, h

    st_f, hc_f = run(ctx, 0, init, False)
    _, hl_f = run(lat, 0, st_f, False)
    st_b, hc_b = run(ctx, 1, init, True)
    _, hl_b = run(lat, 1, st_b, True)
    return hl_f + hl_b, hc_f + hc_b


def mlstm_out(h, o, m_norm_g, dtype):
    B, T = h.shape[:2]
    hn = rmsnorm(h, m_norm_g.reshape(M_HEADS, M_HEAD_DIM)).reshape(B, T, M_WIDTH).astype(dtype)
    return jax.nn.sigmoid(o) * hn


def attn_heads(qa, ka, va, q_norm_g, k_norm_g):
    B, T, _ = qa.shape
    q = rmsnorm(qa.reshape(B, T, A_HEADS, A_HEAD_DIM), q_norm_g)
    k = rmsnorm(ka.reshape(B, T, A_KV_HEADS, A_HEAD_DIM), k_norm_g)
    v = va.reshape(B, T, A_KV_HEADS, A_HEAD_DIM)
    return q, k, v


def attend(qb, k, v):
    s = jnp.einsum('bkgqd,bksd->bkgqs', qb, k).astype(jnp.float32) * (A_HEAD_DIM ** -0.5)
    p = jax.nn.softmax(s, axis=-1).astype(v.dtype)
    return jnp.einsum('bkgqs,bksd->bkgqd', p, v)


def gqa_latent(q, k_all, v_all):
    B, S = q.shape[:2]
    nb = S // Q_BLOCK
    qb = q.reshape(B, nb, Q_BLOCK, A_KV_HEADS, A_GROUPS, A_HEAD_DIM).transpose(1, 0, 3, 4, 2, 5)
    kt = k_all.transpose(0, 2, 1, 3)
    vt = v_all.transpose(0, 2, 1, 3)
    ob = lax.map(lambda blk: attend(blk, kt, vt), qb)
    return ob.transpose(1, 0, 4, 2, 3, 5).reshape(B, S, A_HEADS * A_HEAD_DIM)


def gqa_context(q, k, v):
    B, C = q.shape[:2]
    qb = q.reshape(B, C, A_KV_HEADS, A_GROUPS, A_HEAD_DIM).transpose(0, 2, 3, 1, 4)
    o = attend(qb, k.transpose(0, 2, 1, 3), v.transpose(0, 2, 1, 3))
    return o.transpose(0, 3, 1, 2, 4).reshape(B, C, A_HEADS * A_HEAD_DIM)


def merge_branches(zg, ym, ya, w_pa, w_pb, w_o):
    g_m, g_a = jnp.split(jax.nn.sigmoid(zg), N_BRANCH, axis=-1)
    return (g_m * (ym @ w_pa) + g_a * (ya @ w_pb)) @ w_o


def token_mix(h, hc, w_in, gate_b, conv_w, conv_b, m_norm_g, q_norm_g, k_norm_g, w_pa, w_pb, w_o, cos, sin, need_ctx):
    zl = split_cols(h @ w_in)
    zc = split_cols(hc @ w_in)
    lat_m = mlstm_prep(zl[0], zl[1], zl[2], zl[4], conv_w, conv_b, gate_b)
    ctx_m = mlstm_prep(zc[0], zc[1], zc[2], zc[4], conv_w, conv_b, gate_b)
    hl, hcm = mlstm_two_way(lat_m, ctx_m)
    ym = mlstm_out(hl, zl[3], m_norm_g, h.dtype)
    ql, kl, vl = attn_heads(zl[5], zl[6], zl[7], q_norm_g, k_norm_g)
    qc, kc, vc = attn_heads(zc[5], zc[6], zc[7], q_norm_g, k_norm_g)
    ql = apply_rope(ql, cos, sin)
    kl = apply_rope(kl, cos, sin)
    ya = gqa_latent(ql, jnp.concatenate([kl, kc], axis=1), jnp.concatenate([vl, vc], axis=1))
    y = merge_branches(zl[8], ym, ya, w_pa, w_pb, w_o)
    if not need_ctx:
        return y, None
    ymc = mlstm_out(hcm, zc[3], m_norm_g, hc.dtype)
    yac = gqa_context(qc, kc, vc)
    return y, merge_branches(zc[8], ymc, yac, w_pa, w_pb, w_o)


def swiglu(h, w_g, w_u, w_d):
    return (jax.nn.silu(h @ w_g) * (h @ w_u)) @ w_d


def setup_inputs(seed: int = 0) -> dict:
    key = jax.random.key(seed)
    ks = jax.random.split(key, 24)
    nrm = lambda k, shape, scale: jax.random.normal(k, shape, jnp.float32) * scale
    D, L = D_MODEL, DEPTH
    i_bias = nrm(ks[9], (L, 2, M_HEADS), 0.1)
    f_bias = jnp.linspace(3.0, 6.0, M_HEADS, dtype=jnp.float32)[None, None, :] + nrm(ks[10], (L, 2, M_HEADS), 0.1)
    gate_b = jnp.stack([i_bias, f_bias], axis=2).reshape(L, 4 * M_HEADS)
    return {
        'x': nrm(ks[0], (BATCH, SEQ, D), 1.0),
        'c': nrm(ks[1], (BATCH, D), 1.0),
        'ctx': nrm(ks[2], (BATCH, CTX_LEN, D), 1.0),
        'c_ctx': nrm(ks[3], (D,), 1.0),
        'w_mod': nrm(ks[4], (L, D, 6 * D), 0.5 * D ** -0.5),
        'b_mod': nrm(ks[5], (L, 6 * D), 0.02),
        'norm1_g': 1.0 + nrm(ks[6], (L, D), 0.02),
        'norm2_g': 1.0 + nrm(ks[7], (L, D), 0.02),
        'w_in': nrm(ks[8], (L, D, D_IN), D ** -0.5),
        'gate_b': gate_b,
        'conv_w': nrm(ks[11], (L, CONV_K, 2 * M_WIDTH), CONV_K ** -0.5),
        'conv_b': nrm(ks[12], (L, 2 * M_WIDTH), 0.02),
        'm_norm_g': 1.0 + nrm(ks[13], (L, M_WIDTH), 0.02),
        'q_norm_g': 1.0 + nrm(ks[14], (L, A_HEAD_DIM), 0.02),
        'k_norm_g': 1.0 + nrm(ks[15], (L, A_HEAD_DIM), 0.02),
        'w_pa': nrm(ks[16], (L, M_WIDTH, D), M_WIDTH ** -0.5),
        'w_pb': nrm(ks[17], (L, A_HEADS * A_HEAD_DIM, D), (A_HEADS * A_HEAD_DIM) ** -0.5),
        'w_o': nrm(ks[18], (L, D, D), D ** -0.5),
        'w_ffn_gate': nrm(ks[19], (L, D, D_FF), D ** -0.5),
        'w_ffn_up': nrm(ks[20], (L, D, D_FF), D ** -0.5),
        'w_ffn_down': nrm(ks[21], (L, D_FF, D), D_FF ** -0.5),
        'final_g': 1.0 + nrm(ks[22], (D,), 0.02),
    }


def reference(x, c, ctx, c_ctx, w_mod, b_mod, norm1_g, norm2_g, w_in, gate_b, conv_w, conv_b, m_norm_g, q_norm_g, k_norm_g, w_pa, w_pb, w_o, w_ffn_gate, w_ffn_up, w_ffn_down, final_g):
    B, S, _ = x.shape
    ROWS = S // GRID_W
    cos, sin = axial_rope_tables(ROWS, x.dtype)
    silu_c = jax.nn.silu(c)
    silu_cc = jax.nn.silu(c_ctx)
    for l in range(DEPTH):
        last = l == DEPTH - 1
        mod = (silu_c @ w_mod[l] + b_mod[l])[:, None, :]
        mod_c = silu_cc @ w_mod[l] + b_mod[l]
        sh1, sc1, g1, sh2, sc2, g2 = jnp.split(mod, 6, axis=-1)
        csh1, csc1, cg1, csh2, csc2, cg2 = jnp.split(mod_c, 6, axis=-1)
        h = rmsnorm(x, norm1_g[l]) * (1.0 + sc1) + sh1
        hc = rmsnorm(ctx, norm1_g[l]) * (1.0 + csc1) + csh1
        y, yc = token_mix(h, hc, w_in[l], gate_b[l], conv_w[l], conv_b[l], m_norm_g[l], q_norm_g[l], k_norm_g[l], w_pa[l], w_pb[l], w_o[l], cos, sin, not last)
        x = x + g1 * y
        x = x + g2 * swiglu(rmsnorm(x, norm2_g[l]) * (1.0 + sc2) + sh2, w_ffn_gate[l], w_ffn_up[l], w_ffn_down[l])
        if not last:
            ctx = ctx + cg1 * yc
            ctx = ctx + cg2 * swiglu(rmsnorm(ctx, norm2_g[l]) * (1.0 + csc2) + csh2, w_ffn_gate[l], w_ffn_up[l], w_ffn_down[l])
    return rmsnorm(x, final_g)
```

```python
import functools

import jax
import jax.numpy as jnp
from jax import lax
from jax.experimental import pallas as pl
from jax.experimental.pallas import tpu as pltpu

F32 = jnp.float32
BF16 = jnp.bfloat16

EPS = 1e-6
GRID_W = 64
ROPE_THETA = 10000.0
M_HEADS = 4
A_HEADS = 8
A_KV_HEADS = 2
A_GROUPS = A_HEADS // A_KV_HEADS
CONV_K = 5
CONV_PAD = CONV_K // 2

V7X_VMEM_BYTES = 64 * 1024 * 1024
VMEM_LIMIT = V7X_VMEM_BYTES - 8 * 1024 * 1024
BF16_SUBLANES = 16

INPROJ_ROWS = 256
CONV_ROWS = 512
MLSTM_CHUNK = 128
ATTN_Q_ROWS = 128
ATTN_K_ROWS = 512
TAIL_ROWS = 256


def _params(*sem):
    return pltpu.CompilerParams(dimension_semantics=sem, vmem_limit_bytes=VMEM_LIMIT)


def _resident(shape):
    zeros = (0,) * len(shape)
    return pl.BlockSpec(shape, lambda *_: zeros, pipeline_mode=pl.Buffered(1))


def _sigmoid(x):
    return 1.0 / (1.0 + jnp.exp(-x))


def _log_sigmoid(x):
    return jnp.minimum(x, 0.0) - jnp.log(1.0 + jnp.exp(-jnp.abs(x)))


def _rms(x, width):
    return x * lax.rsqrt(jnp.sum(x * x, axis=-1, keepdims=True) * (1.0 / width) + EPS)


def _dot(a, b):
    return jnp.dot(a, b, preferred_element_type=F32)


def _dot_nt(a, b):
    return lax.dot_general(a, b, (((1,), (1,)), ((), ())), preferred_element_type=F32)


def _dot_tn(a, b):
    return lax.dot_general(a, b, (((0,), (0,)), ((), ())), preferred_element_type=F32)


def _mod_kernel(c_ref, w_ref, b_ref, o_ref):
    c = c_ref[...]
    s = c * _sigmoid(c)
    o_ref[...] = jnp.dot(s, w_ref[...], preferred_element_type=F32, precision=lax.Precision.HIGHEST) + b_ref[...]


def _mod(c_rows, w_mod, b_mod):
    rows, d = c_rows.shape
    n = w_mod.shape[1]
    return pl.pallas_call(
        _mod_kernel,
        grid=(n // d,),
        in_specs=[
            pl.BlockSpec((rows, d), lambda j: (0, 0)),
            pl.BlockSpec((d, d), lambda j: (0, j)),
            pl.BlockSpec((1, d), lambda j: (0, j)),
        ],
        out_specs=pl.BlockSpec((rows, d), lambda j: (0, j)),
        out_shape=jax.ShapeDtypeStruct((rows, n), F32),
        compiler_params=_params("parallel"),
        name="mod",
    )(c_rows, w_mod, b_mod.reshape(1, n))


def _inproj_kernel(x_ref, sc_ref, sh_ref, g_ref, cos_ref, sin_ref, gq_ref, gk_ref, gb_ref,
                   w_mqk, w_mv, w_mo, w_gt, w_aq, w_ak, w_av, w_mg,
                   qk_ref, v_ref, og_ref, gt_ref, aq_ref, ak_ref, av_ref, mg_ref, *, rope):
    x = x_ref[0]
    d = x.shape[-1]
    h = (_rms(x, d) * g_ref[...]) * (1.0 + sc_ref[0]) + sh_ref[0]
    hb = h.astype(BF16)

    qk_ref[0] = _dot(hb, w_mqk[...]).astype(BF16)
    v_ref[0] = _dot(hb, w_mv[...]).astype(BF16)
    og_ref[0] = _sigmoid(_dot(hb, w_mo[...])).astype(BF16)
    mg_ref[0] = _sigmoid(_dot(hb, w_mg[...])).astype(BF16)
    av_ref[0] = _dot(hb, w_av[...]).astype(BF16)

    gt = _dot_nt(w_gt[...], hb) + gb_ref[...]
    row = lax.broadcasted_iota(jnp.int32, gt.shape, 0)
    is_forget = (row // M_HEADS) % 2 == 1
    gt_ref[0] = jnp.where(is_forget, _log_sigmoid(gt), gt)

    def heads(z, gain, out_ref):
        hd = gain.shape[-1]
        for i in range(z.shape[-1] // hd):
            zh = _rms(z[:, i * hd:(i + 1) * hd], hd) * gain
            if rope:
                zh = zh * cos_ref[...] + pltpu.roll(zh, hd // 2, axis=1) * sin_ref[...]
            out_ref[0, :, i * hd:(i + 1) * hd] = zh.astype(BF16)

    heads(_dot(hb, w_aq[...]), gq_ref[...], aq_ref)
    heads(_dot(hb, w_ak[...]), gk_ref[...], ak_ref)


def _inproj(x, scale, shift, gain, cosf, sinf, gq, gk, gate_b, weights, *, rope):
    b, t, d = x.shape
    tm = min(INPROJ_ROWS, t)
    hd = gq.shape[-1]
    widths = [w.shape[1] for w in weights]
    n_mqk, n_mv, n_mo, n_gt, n_aq, n_ak, n_av, n_mg = widths
    w_gt_t = weights[3].T
    ws = list(weights)
    ws[3] = w_gt_t
    row = lambda n: pl.BlockSpec((1, tm, n), lambda i, j: (i, j, 0))
    per_b = pl.BlockSpec((1, 1, d), lambda i, j: (i, 0, 0))
    table = pl.BlockSpec((tm, hd), lambda i, j: (j, 0))
    out_shape = [
        jax.ShapeDtypeStruct((b, t, n_mqk), BF16),
        jax.ShapeDtypeStruct((b, t, n_mv), BF16),
        jax.ShapeDtypeStruct((b, t, n_mo), BF16),
        jax.ShapeDtypeStruct((b, n_gt, t), F32),
        jax.ShapeDtypeStruct((b, t, n_aq), BF16),
        jax.ShapeDtypeStruct((b, t, n_ak), BF16),
        jax.ShapeDtypeStruct((b, t, n_av), BF16),
        jax.ShapeDtypeStruct((b, t, n_mg), BF16),
    ]
    out_specs = [row(n_mqk), row(n_mv), row(n_mo),
                 pl.BlockSpec((1, n_gt, tm), lambda i, j: (i, 0, j)),
                 row(n_aq), row(n_ak), row(n_av), row(n_mg)]
    return pl.pallas_call(
        functools.partial(_inproj_kernel, rope=rope),
        grid=(b, t // tm),
        in_specs=[row(d), per_b, per_b, _resident((1, d)), table, table,
                  _resident((1, hd)), _resident((1, hd)), _resident((n_gt, 1))]
                 + [_resident(w.shape) for w in ws],
        out_specs=out_specs,
        out_shape=out_shape,
        compiler_params=_params("parallel", "parallel"),
        name="inproj_rope" if rope else "inproj_ctx",
    )(x, scale, shift, gain, cosf, sinf, gq, gk, gate_b, *ws)


def _conv_kernel(prev_ref, cur_ref, next_ref, w_ref, b_ref, q_ref, k_ref, *, k_scale):
    j = pl.program_id(1)
    last = pl.num_programs(1) - 1
    cur = cur_ref[0].astype(F32)
    tm = cur.shape[0]
    halo = prev_ref.shape[1]
    prev = jnp.where(j > 0, prev_ref[0].astype(F32), 0.0)
    nxt = jnp.where(j < last, next_ref[0].astype(F32), 0.0)
    ext = jnp.concatenate([prev, cur, nxt], axis=0)
    acc = jnp.zeros_like(cur) + b_ref[...]
    for tap in range(CONV_K):
        start = halo + tap - CONV_PAD
        acc = acc + w_ref[tap:tap + 1, :] * ext[start:start + tm, :]
    y = acc * _sigmoid(acc)
    half = y.shape[-1] // 2
    q_ref[0] = y[:, :half].astype(BF16)
    k_ref[0] = (y[:, half:] * k_scale).astype(BF16)


def _conv(qk, conv_w, conv_b, k_scale):
    b, t, c = qk.shape
    tm = min(CONV_ROWS, t)
    halo = BF16_SUBLANES
    per = tm // halo
    nblk = t // halo
    out = jax.ShapeDtypeStruct((b, t, c // 2), BF16)
    return pl.pallas_call(
        functools.partial(_conv_kernel, k_scale=k_scale),
        grid=(b, t // tm),
        in_specs=[
            pl.BlockSpec((1, halo, c), lambda i, j: (i, jnp.maximum(j * per - 1, 0), 0)),
            pl.BlockSpec((1, tm, c), lambda i, j: (i, j, 0)),
            pl.BlockSpec((1, halo, c), lambda i, j: (i, jnp.minimum((j + 1) * per, nblk - 1), 0)),
            _resident(conv_w.shape), _resident((1, c)),
        ],
        out_specs=[pl.BlockSpec((1, tm, c // 2), lambda i, j: (i, j, 0))] * 2,
        out_shape=[out, out],
        compiler_params=_params("parallel", "parallel"),
        name="conv",
    )(qk, qk, qk, conv_w, conv_b.reshape(1, c))


def _mlstm_kernel(*refs, emit_h):
    (qf_ref, kf_ref, vf_ref, gf_ref, qb_ref, kb_ref, vb_ref, gb_ref, c0_ref, n0_ref, m0_ref) = refs[:11]
    if emit_h:
        hf_ref, hb_ref, c_ref, n_ref, m_ref = refs[11:]
    else:
        hf_ref = hb_ref = None
        c_ref, n_ref, m_ref = refs[11:]

    @pl.when(pl.program_id(1) == 0)
    def _():
        c_ref[...] = c0_ref[...]
        n_ref[...] = n0_ref[...]
        m_ref[...] = m0_ref[...]

    chunk = qf_ref.shape[1]
    hd = qf_ref.shape[2] // M_HEADS
    t_idx = lax.broadcasted_iota(jnp.int32, (chunk, chunk), 0)
    s_idx = lax.broadcasted_iota(jnp.int32, (chunk, chunk), 1)
    eye = s_idx == t_idx
    neg_inf = -jnp.inf

    streams = ((qf_ref, kf_ref, vf_ref, gf_ref, hf_ref), (qb_ref, kb_ref, vb_ref, gb_ref, hb_ref))
    for direction, (q_ref, k_ref, v_ref, g_ref, h_ref) in enumerate(streams):
        seen = s_idx <= t_idx if direction == 0 else s_idx >= t_idx
        gates = g_ref[0]
        upto = t_idx <= s_idx if direction == 0 else t_idx >= s_idx
        cum = jnp.dot(gates, jnp.where(upto, 1.0, 0.0), preferred_element_type=F32,
                      precision=lax.Precision.HIGHEST)
        for head in range(M_HEADS):
            chain = head * 2 + direction
            r_i = (2 * direction) * M_HEADS + head
            r_f = (2 * direction + 1) * M_HEADS + head
            lf_row = gates[r_f:r_f + 1, :]
            b_row = gates[r_i:r_i + 1, :] - cum[r_f:r_f + 1, :]
            a_col = jnp.sum(jnp.where(seen, lf_row, 0.0), axis=-1, keepdims=True)
            cmb_col = jnp.max(jnp.where(seen, b_row, neg_inf), axis=-1, keepdims=True)
            b_col = jnp.sum(jnp.where(eye, b_row, 0.0), axis=-1, keepdims=True)
            f_total = jnp.sum(lf_row, axis=-1, keepdims=True)
            b_max = jnp.max(b_row, axis=-1, keepdims=True)

            lanes = slice(head * hd, (head + 1) * hd)
            q = q_ref[0, :, lanes]
            k = k_ref[0, :, lanes]
            v = v_ref[0, :, lanes]
            c_state = c_ref[0, chain]
            n_state = n_ref[0, chain]
            m_state = m_ref[0, chain][:, :1]

            if emit_h:
                u_col = jnp.maximum(m_state, cmb_col)
                p = jnp.exp(jnp.where(seen, b_row - u_col, neg_inf))
                sp = _dot_nt(q, k) * p
                w_inter = jnp.exp(m_state - u_col)
                num = w_inter * _dot(q, c_state.astype(BF16)) + _dot(sp.astype(BF16), v)
                qn = jnp.sum(q.astype(F32) * n_state, axis=-1, keepdims=True)
                den = w_inter * qn + jnp.sum(sp, axis=-1, keepdims=True)
                hval = num / jnp.maximum(jnp.abs(den), jnp.exp(-(a_col + u_col)))
                h_ref[0, :, lanes] = hval.astype(BF16)

            u_last = jnp.maximum(m_state, b_max)
            a_prev = jnp.exp(m_state - u_last)
            kw = k.astype(F32) * jnp.exp(b_col - u_last)
            c_ref[0, chain] = a_prev * c_state + _dot_tn(kw.astype(BF16), v)
            n_ref[0, chain] = a_prev * n_state + jnp.sum(kw, axis=0, keepdims=True)
            m_ref[0, chain] = jnp.broadcast_to(f_total + u_last, m_ref.shape[2:])


def _mlstm(q, k, v, gt, state, *, emit_h):
    b, t, width = q.shape
    chunk = min(MLSTM_CHUNK, t)
    nc = t // chunk
    hd = width // M_HEADS
    chains = 2 * M_HEADS
    fwd = pl.BlockSpec((1, chunk, width), lambda i, j: (i, j, 0))
    bwd = pl.BlockSpec((1, chunk, width), lambda i, j: (i, nc - 1 - j, 0))
    g_fwd = pl.BlockSpec((1, gt.shape[1], chunk), lambda i, j: (i, 0, j))
    g_bwd = pl.BlockSpec((1, gt.shape[1], chunk), lambda i, j: (i, 0, nc - 1 - j))
    c_spec = pl.BlockSpec((1, chains, hd, hd), lambda i, j: (i, 0, 0, 0))
    n_spec = pl.BlockSpec((1, chains, 1, hd), lambda i, j: (i, 0, 0, 0))
    m_spec = pl.BlockSpec((1, chains, 1, 128), lambda i, j: (i, 0, 0, 0))
    state_shapes = [jax.ShapeDtypeStruct((b, chains, hd, hd), F32),
                    jax.ShapeDtypeStruct((b, chains, 1, hd), F32),
                    jax.ShapeDtypeStruct((b, chains, 1, 128), F32)]
    h_shapes = [jax.ShapeDtypeStruct((b, t, width), BF16)] * 2 if emit_h else []
    h_specs = [fwd, bwd] if emit_h else []
    return pl.pallas_call(
        functools.partial(_mlstm_kernel, emit_h=emit_h),
        grid=(b, nc),
        in_specs=[fwd, fwd, fwd, g_fwd, bwd, bwd, bwd, g_bwd, c_spec, n_spec, m_spec],
        out_specs=h_specs + [c_spec, n_spec, m_spec],
        out_shape=h_shapes + state_shapes,
        compiler_params=_params("parallel", "arbitrary"),
        name="mlstm" if emit_h else "mlstm_ctx",
    )(q, k, v, gt, q, k, v, gt, *state)


def _attn_kernel(q_ref, kl_ref, vl_ref, kc_ref, vc_ref, o_ref, *, k_rows):
    tq = q_ref.shape[1]
    hd = kl_ref.shape[2]
    q = q_ref[0]
    qs = jnp.concatenate([q[:, i * hd:(i + 1) * hd] for i in range(A_GROUPS)], axis=0)
    rows = qs.shape[0]

    def update(carry, k, v):
        m, l, acc = carry
        s = _dot_nt(qs, k)
        m_new = jnp.maximum(m, jnp.max(s, axis=-1, keepdims=True))
        alpha = jnp.exp(m - m_new)
        p = jnp.exp(s - m_new)
        l = alpha * l + jnp.sum(p, axis=-1, keepdims=True)
        acc = alpha * acc + _dot(p.astype(BF16), v)
        return m_new, l, acc

    carry = (jnp.full((rows, 1), -jnp.inf, F32), jnp.zeros((rows, 1), F32), jnp.zeros((rows, hd), F32))
    carry = update(carry, kc_ref[0], vc_ref[0])

    def body(c, carry):
        start = pl.multiple_of(c * k_rows, k_rows)
        return update(carry, kl_ref[0, pl.ds(start, k_rows), :], vl_ref[0, pl.ds(start, k_rows), :])

    _, l, acc = lax.fori_loop(0, kl_ref.shape[1] // k_rows, body, carry)
    out = acc / l
    for i in range(A_GROUPS):
        o_ref[0, :, i * hd:(i + 1) * hd] = out[i * tq:(i + 1) * tq, :].astype(BF16)


def _attn(q, k_lat, v_lat, k_ctx, v_ctx):
    b, s, width = q.shape
    hd = width // A_HEADS
    group = A_GROUPS * hd
    tq = min(ATTN_Q_ROWS, s)
    k_rows = min(ATTN_K_ROWS, s)
    ctx = k_ctx.shape[1]
    q_spec = pl.BlockSpec((1, tq, group), lambda i, g, j: (i, j, g))
    lat = pl.BlockSpec((1, s, hd), lambda i, g, j: (i, 0, g))
    cx = pl.BlockSpec((1, ctx, hd), lambda i, g, j: (i, 0, g))
    return pl.pallas_call(
        functools.partial(_attn_kernel, k_rows=k_rows),
        grid=(b, A_KV_HEADS, s // tq),
        in_specs=[q_spec, lat, lat, cx, cx],
        out_specs=q_spec,
        out_shape=jax.ShapeDtypeStruct((b, s, width), BF16),
        compiler_params=_params("parallel", "parallel", "arbitrary"),
        name="attn",
    )(q, k_lat, v_lat, k_ctx, v_ctx)


def _tail_kernel(x_ref, hf_ref, hb_ref, og_ref, ya_ref, mg_ref, g1_ref, sc2_ref, sh2_ref, g2_ref,
                 gm_ref, n2_ref, fg_ref, w_pa, w_pb, w_o, w_g, w_u, w_d, o_ref):
    x = x_ref[0]
    d = x.shape[-1]
    hd = d // M_HEADS
    hsum = hf_ref[0].astype(F32) + hb_ref[0].astype(F32)
    hn = jnp.concatenate([_rms(hsum[:, i * hd:(i + 1) * hd], hd) for i in range(M_HEADS)], axis=-1)
    ym = (og_ref[0].astype(F32) * (hn * gm_ref[...])).astype(BF16)
    mg = mg_ref[0].astype(F32)
    merged = mg[:, :d] * _dot(ym, w_pa[...]) + mg[:, d:] * _dot(ya_ref[0], w_pb[...])
    x1 = x + g1_ref[0] * _dot(merged.astype(BF16), w_o[...])
    h2 = ((_rms(x1, d) * n2_ref[...]) * (1.0 + sc2_ref[0]) + sh2_ref[0]).astype(BF16)
    gate = _dot(h2, w_g[...])
    ff = _dot(((gate * _sigmoid(gate)) * _dot(h2, w_u[...])).astype(BF16), w_d[...])
    x2 = x1 + g2_ref[0] * ff
    o_ref[0] = _rms(x2, d) * fg_ref[...]


def _tail(x, hf, hb, og, ya, mg, g1, sc2, sh2, g2, gm, n2, fg, weights):
    b, t, d = x.shape
    tm = min(TAIL_ROWS, t)
    row = lambda n: pl.BlockSpec((1, tm, n), lambda i, j: (i, j, 0))
    per_b = pl.BlockSpec((1, 1, d), lambda i, j: (i, 0, 0))
    vec = _resident((1, d))
    return pl.pallas_call(
        _tail_kernel,
        grid=(b, t // tm),
        in_specs=[row(d), row(d), row(d), row(d), row(d), row(2 * d), per_b, per_b, per_b, per_b,
                  vec, vec, vec] + [_resident(w.shape) for w in weights],
        out_specs=row(d),
        out_shape=jax.ShapeDtypeStruct((b, t, d), F32),
        compiler_params=_params("parallel", "parallel"),
        name="tail",
    )(x, hf, hb, og, ya, mg, g1, sc2, sh2, g2, gm, n2, fg, *weights)


def _rope_tables(seq, head_dim):
    pairs = head_dim // 4
    pos = jnp.arange(seq, dtype=jnp.int32)
    row = (pos // GRID_W).astype(F32)
    col = (pos % GRID_W).astype(F32)
    inv = ROPE_THETA ** (-jnp.arange(pairs, dtype=F32) / pairs)
    ang = jnp.concatenate([row[:, None] * inv, col[:, None] * inv], axis=-1)
    cos, sin = jnp.cos(ang), jnp.sin(ang)
    return jnp.concatenate([cos, cos], axis=-1), jnp.concatenate([-sin, sin], axis=-1)


def _pair_split(head_dim):
    return jnp.concatenate([jnp.arange(0, head_dim, 2), jnp.arange(1, head_dim, 2)])


def kernel(x, c, ctx, c_ctx, w_mod, b_mod, norm1_g, norm2_g, w_in, gate_b, conv_w, conv_b, m_norm_g, q_norm_g,
           k_norm_g, w_pa, w_pb, w_o, w_ffn_gate, w_ffn_up, w_ffn_down, final_g):
    b, s, d = x.shape
    depth = w_mod.shape[0]
    assert depth == 1, "single trunk layer"
    m_hd = d // M_HEADS
    a_hd = d // A_HEADS
    kv_w = A_KV_HEADS * a_hd
    l = 0

    pad_rows = 8 * pl.cdiv(b + 1, 8) - (b + 1)
    c_rows = jnp.concatenate([c, c_ctx[None, :], jnp.zeros((pad_rows, d), F32)], axis=0)
    mod = _mod(c_rows, w_mod[l], b_mod[l])
    sh1, sc1, g1, sh2, sc2, g2 = [mod[:b, i * d:(i + 1) * d].reshape(b, 1, d) for i in range(6)]
    csh1, csc1 = [jnp.broadcast_to(mod[b, i * d:(i + 1) * d], (b, 1, d)) for i in range(2)]

    splits = (d, d, d, d, 4 * M_HEADS, d, kv_w, kv_w, 2 * d)
    offs = [0]
    for n in splits:
        offs.append(offs[-1] + n)
    cols = [w_in[l][:, offs[i]:offs[i + 1]] for i in range(len(splits))]
    perm = _pair_split(a_hd)
    perm_q = (jnp.arange(A_HEADS)[:, None] * a_hd + perm[None, :]).reshape(-1)
    perm_k = (jnp.arange(A_KV_HEADS)[:, None] * a_hd + perm[None, :]).reshape(-1)
    weights = [jnp.concatenate([cols[0], cols[1]], axis=1), cols[2], cols[3], cols[4],
               cols[5][:, perm_q], cols[6][:, perm_k], cols[7], cols[8]]
    weights = [w.astype(BF16) for w in weights]
    gq = (q_norm_g[l][perm] * (a_hd ** -0.5)).reshape(1, a_hd)
    gk = k_norm_g[l][perm].reshape(1, a_hd)
    gb = gate_b[l].reshape(-1, 1)
    cosf, sinf = _rope_tables(s, a_hd)
    n1 = norm1_g[l].reshape(1, d)

    lat = _inproj(x, sc1, sh1, n1, cosf, sinf, gq, gk, gb, weights, rope=True)
    cx = _inproj(ctx, csc1, csh1, n1, cosf, sinf, gq, gk, gb, weights, rope=False)
    qk_l, v_l, og_l, gt_l, aq_l, ak_l, av_l, mg_l = lat
    qk_c, v_c, _, gt_c, _, ak_c, av_c, _ = cx

    k_scale = m_hd ** -0.5
    q_c, k_c = _conv(qk_c, conv_w[l], conv_b[l], k_scale)
    q_l, k_l = _conv(qk_l, conv_w[l], conv_b[l], k_scale)
    chains = 2 * M_HEADS
    init = (jnp.zeros((b, chains, m_hd, m_hd), F32), jnp.zeros((b, chains, 1, m_hd), F32),
            jnp.full((b, chains, 1, 128), -jnp.inf, F32))
    ctx_state = _mlstm(q_c, k_c, v_c, gt_c, init, emit_h=False)
    hf, hb = _mlstm(q_l, k_l, v_l, gt_l, ctx_state, emit_h=True)[:2]

    ya = _attn(aq_l, ak_l, av_l, ak_c, av_c)

    tail_w = [w.astype(BF16) for w in (w_pa[l], w_pb[l], w_o[l], w_ffn_gate[l], w_ffn_up[l], w_ffn_down[l])]
    return _tail(x, hf, hb, og_l, ya, mg_l, g1, sc2, sh2, g2, m_norm_g[l].reshape(1, d),
                 norm2_g[l].reshape(1, d), final_g.reshape(1, d), tail_w)
```

```python
import functools

import jax
import jax.numpy as jnp
from jax import lax
from jax.experimental import pallas as pl
from jax.experimental.pallas import tpu as pltpu

F32 = jnp.float32
BF16 = jnp.bfloat16

EPS = 1e-6
GRID_W = 64
ROPE_THETA = 10000.0
M_HEADS = 4
A_HEADS = 8
A_KV_HEADS = 2
A_GROUPS = A_HEADS // A_KV_HEADS
CONV_K = 5
CONV_PAD = CONV_K // 2
LOG2_E = 1.4426950408889634
BF16_ROUNDING_MARGIN = (1.0 + 2.0 ** -7) ** 2
ATTN_SAFE_LOG2_SPAN = 96.0

V7X_VMEM_BYTES = 64 * 1024 * 1024
VMEM_LIMIT = V7X_VMEM_BYTES - 8 * 1024 * 1024
BF16_SUBLANES = 16

INPROJ_ROWS = 256
CONV_ROWS = 512
MLSTM_CHUNK = 128
ATTN_Q_ROWS = 256
ATTN_K_ROWS = 512
ATTN_UNROLL = 16
TAIL_ROWS = 256


def _params(*sem):
    return pltpu.CompilerParams(dimension_semantics=sem, vmem_limit_bytes=VMEM_LIMIT)


def _resident(shape):
    zeros = (0,) * len(shape)
    return pl.BlockSpec(shape, lambda *_: zeros, pipeline_mode=pl.Buffered(1))


def _sigmoid(x):
    return 1.0 / (1.0 + jnp.exp(-x))


def _log_sigmoid(x):
    return jnp.minimum(x, 0.0) - jnp.log(1.0 + jnp.exp(-jnp.abs(x)))


def _rms(x, width):
    return x * lax.rsqrt(jnp.sum(x * x, axis=-1, keepdims=True) * (1.0 / width) + EPS)


def _dot(a, b):
    return jnp.dot(a, b, preferred_element_type=F32)


def _dot_nt(a, b):
    return lax.dot_general(a, b, (((1,), (1,)), ((), ())), preferred_element_type=F32)


def _dot_tn(a, b):
    return lax.dot_general(a, b, (((0,), (0,)), ((), ())), preferred_element_type=F32)


def _mod_kernel(c_ref, w_ref, b_ref, o_ref):
    c = c_ref[...]
    s = c * _sigmoid(c)
    o_ref[...] = jnp.dot(s, w_ref[...], preferred_element_type=F32, precision=lax.Precision.HIGHEST) + b_ref[...]


def _mod(c_rows, w_mod, b_mod):
    rows, d = c_rows.shape
    n = w_mod.shape[1]
    return pl.pallas_call(
        _mod_kernel,
        grid=(n // d,),
        in_specs=[
            pl.BlockSpec((rows, d), lambda j: (0, 0)),
            pl.BlockSpec((d, d), lambda j: (0, j)),
            pl.BlockSpec((1, d), lambda j: (0, j)),
        ],
        out_specs=pl.BlockSpec((rows, d), lambda j: (0, j)),
        out_shape=jax.ShapeDtypeStruct((rows, n), F32),
        compiler_params=_params("parallel"),
        name="mod",
    )(c_rows, w_mod, b_mod.reshape(1, n))


def _inproj_kernel(x_ref, sc_ref, sh_ref, g_ref, cos_ref, sin_ref, gq_ref, gk_ref, gb_ref,
                   w_mqk, w_mv, w_mo, w_gt, w_aq, w_ak, w_av, w_mg,
                   qk_ref, v_ref, og_ref, gt_ref, aq_ref, ak_ref, av_ref, mg_ref, *, rope):
    x = x_ref[0]
    d = x.shape[-1]
    h = (_rms(x, d) * g_ref[...]) * (1.0 + sc_ref[0]) + sh_ref[0]
    hb = h.astype(BF16)

    qk_ref[0] = _dot(hb, w_mqk[...]).astype(BF16)
    v_ref[0] = _dot(hb, w_mv[...]).astype(BF16)
    og_ref[0] = _sigmoid(_dot(hb, w_mo[...])).astype(BF16)
    mg_ref[0] = _sigmoid(_dot(hb, w_mg[...])).astype(BF16)
    av = _dot(hb, w_av[...]).astype(BF16)
    hd = gk_ref.shape[-1]
    ones_col = jnp.where(lax.broadcasted_iota(jnp.int32, (av.shape[0], hd), 1) == 0, 1.0, 0.0).astype(BF16)
    for i in range(av.shape[-1] // hd):
        av_ref[0, :, 2 * i * hd:(2 * i + 1) * hd] = av[:, i * hd:(i + 1) * hd]
        av_ref[0, :, (2 * i + 1) * hd:(2 * i + 2) * hd] = ones_col

    gt = _dot_nt(w_gt[...], hb) + gb_ref[...]
    row = lax.broadcasted_iota(jnp.int32, gt.shape, 0)
    is_forget = (row // M_HEADS) % 2 == 1
    gt_ref[0] = jnp.where(is_forget, _log_sigmoid(gt), gt)

    def heads(z, gain, out_ref):
        hd = gain.shape[-1]
        for i in range(z.shape[-1] // hd):
            zh = _rms(z[:, i * hd:(i + 1) * hd], hd) * gain
            if rope:
                zh = zh * cos_ref[...] + pltpu.roll(zh, hd // 2, axis=1) * sin_ref[...]
            out_ref[0, :, i * hd:(i + 1) * hd] = zh.astype(BF16)

    heads(_dot(hb, w_aq[...]), gq_ref[...], aq_ref)
    heads(_dot(hb, w_ak[...]), gk_ref[...], ak_ref)


def _inproj(x, scale, shift, gain, cosf, sinf, gq, gk, gate_b, weights, *, rope):
    b, t, d = x.shape
    tm = min(INPROJ_ROWS, t)
    hd = gq.shape[-1]
    widths = [w.shape[1] for w in weights]
    n_mqk, n_mv, n_mo, n_gt, n_aq, n_ak, n_av, n_mg = widths
    w_gt_t = weights[3].T
    ws = list(weights)
    ws[3] = w_gt_t
    row = lambda n: pl.BlockSpec((1, tm, n), lambda i, j: (i, j, 0))
    per_b = pl.BlockSpec((1, 1, d), lambda i, j: (i, 0, 0))
    table = pl.BlockSpec((tm, hd), lambda i, j: (j, 0))
    out_shape = [
        jax.ShapeDtypeStruct((b, t, n_mqk), BF16),
        jax.ShapeDtypeStruct((b, t, n_mv), BF16),
        jax.ShapeDtypeStruct((b, t, n_mo), BF16),
        jax.ShapeDtypeStruct((b, n_gt, t), F32),
        jax.ShapeDtypeStruct((b, t, n_aq), BF16),
        jax.ShapeDtypeStruct((b, t, n_ak), BF16),
        jax.ShapeDtypeStruct((b, t, 2 * n_av), BF16),
        jax.ShapeDtypeStruct((b, t, n_mg), BF16),
    ]
    out_specs = [row(n_mqk), row(n_mv), row(n_mo),
                 pl.BlockSpec((1, n_gt, tm), lambda i, j: (i, 0, j)),
                 row(n_aq), row(n_ak), row(2 * n_av), row(n_mg)]
    return pl.pallas_call(
        functools.partial(_inproj_kernel, rope=rope),
        grid=(b, t // tm),
        in_specs=[row(d), per_b, per_b, _resident((1, d)), table, table,
                  _resident((1, hd)), _resident((1, hd)), _resident((n_gt, 1))]
                 + [_resident(w.shape) for w in ws],
        out_specs=out_specs,
        out_shape=out_shape,
        compiler_params=_params("parallel", "parallel"),
        name="inproj_rope" if rope else "inproj_ctx",
    )(x, scale, shift, gain, cosf, sinf, gq, gk, gate_b, *ws)


def _conv_kernel(prev_ref, cur_ref, next_ref, w_ref, b_ref, q_ref, k_ref, *, k_scale):
    j = pl.program_id(1)
    last = pl.num_programs(1) - 1
    cur = cur_ref[0].astype(F32)
    tm = cur.shape[0]
    halo = prev_ref.shape[1]
    prev = jnp.where(j > 0, prev_ref[0].astype(F32), 0.0)
    nxt = jnp.where(j < last, next_ref[0].astype(F32), 0.0)
    ext = jnp.concatenate([prev, cur, nxt], axis=0)
    acc = jnp.zeros_like(cur) + b_ref[...]
    for tap in range(CONV_K):
        start = halo + tap - CONV_PAD
        acc = acc + w_ref[tap:tap + 1, :] * ext[start:start + tm, :]
    y = acc * _sigmoid(acc)
    half = y.shape[-1] // 2
    q_ref[0] = y[:, :half].astype(BF16)
    k_ref[0] = (y[:, half:] * k_scale).astype(BF16)


def _conv(qk, conv_w, conv_b, k_scale):
    b, t, c = qk.shape
    tm = min(CONV_ROWS, t)
    halo = BF16_SUBLANES
    per = tm // halo
    nblk = t // halo
    out = jax.ShapeDtypeStruct((b, t, c // 2), BF16)
    return pl.pallas_call(
        functools.partial(_conv_kernel, k_scale=k_scale),
        grid=(b, t // tm),
        in_specs=[
            pl.BlockSpec((1, halo, c), lambda i, j: (i, jnp.maximum(j * per - 1, 0), 0)),
            pl.BlockSpec((1, tm, c), lambda i, j: (i, j, 0)),
            pl.BlockSpec((1, halo, c), lambda i, j: (i, jnp.minimum((j + 1) * per, nblk - 1), 0)),
            _resident(conv_w.shape), _resident((1, c)),
        ],
        out_specs=[pl.BlockSpec((1, tm, c // 2), lambda i, j: (i, j, 0))] * 2,
        out_shape=[out, out],
        compiler_params=_params("parallel", "parallel"),
        name="conv",
    )(qk, qk, qk, conv_w, conv_b.reshape(1, c))


def _mlstm_kernel(*refs, emit_h):
    (qf_ref, kf_ref, vf_ref, gf_ref, qb_ref, kb_ref, vb_ref, gb_ref, c0_ref, n0_ref, m0_ref) = refs[:11]
    if emit_h:
        hf_ref, hb_ref, c_ref, n_ref, m_ref = refs[11:]
    else:
        hf_ref = hb_ref = None
        c_ref, n_ref, m_ref = refs[11:]

    @pl.when(pl.program_id(1) == 0)
    def _():
        c_ref[...] = c0_ref[...]
        n_ref[...] = n0_ref[...]
        m_ref[...] = m0_ref[...]

    chunk = qf_ref.shape[1]
    hd = qf_ref.shape[2] // M_HEADS
    t_idx = lax.broadcasted_iota(jnp.int32, (chunk, chunk), 0)
    s_idx = lax.broadcasted_iota(jnp.int32, (chunk, chunk), 1)
    eye = s_idx == t_idx
    neg_inf = -jnp.inf

    streams = ((qf_ref, kf_ref, vf_ref, gf_ref, hf_ref), (qb_ref, kb_ref, vb_ref, gb_ref, hb_ref))
    for direction, (q_ref, k_ref, v_ref, g_ref, h_ref) in enumerate(streams):
        seen = s_idx <= t_idx if direction == 0 else s_idx >= t_idx
        gates = g_ref[0]
        upto = t_idx <= s_idx if direction == 0 else t_idx >= s_idx
        cum = jnp.dot(gates, jnp.where(upto, 1.0, 0.0), preferred_element_type=F32,
                      precision=lax.Precision.HIGHEST)
        for head in range(M_HEADS):
            chain = head * 2 + direction
            r_i = (2 * direction) * M_HEADS + head
            r_f = (2 * direction + 1) * M_HEADS + head
            lf_row = gates[r_f:r_f + 1, :]
            b_row = gates[r_i:r_i + 1, :] - cum[r_f:r_f + 1, :]
            a_col = jnp.sum(jnp.where(seen, lf_row, 0.0), axis=-1, keepdims=True)
            cmb_col = jnp.max(jnp.where(seen, b_row, neg_inf), axis=-1, keepdims=True)
            b_col = jnp.sum(jnp.where(eye, b_row, 0.0), axis=-1, keepdims=True)
            f_total = jnp.sum(lf_row, axis=-1, keepdims=True)
            b_max = jnp.max(b_row, axis=-1, keepdims=True)

            lanes = slice(head * hd, (head + 1) * hd)
            q = q_ref[0, :, lanes]
            k = k_ref[0, :, lanes]
            v = v_ref[0, :, lanes]
            c_state = c_ref[0, chain]
            n_state = n_ref[0, chain]
            m_state = m_ref[0, chain][:, :1]

            if emit_h:
                u_col = jnp.maximum(m_state, cmb_col)
                p = jnp.exp(jnp.where(seen, b_row - u_col, neg_inf))
                sp = _dot_nt(q, k) * p
                w_inter = jnp.exp(m_state - u_col)
                num = w_inter * _dot(q, c_state.astype(BF16)) + _dot(sp.astype(BF16), v)
                qn = jnp.sum(q.astype(F32) * n_state, axis=-1, keepdims=True)
                den = w_inter * qn + jnp.sum(sp, axis=-1, keepdims=True)
                hval = num / jnp.maximum(jnp.abs(den), jnp.exp(-(a_col + u_col)))
                h_ref[0, :, lanes] = hval.astype(BF16)

            u_last = jnp.maximum(m_state, b_max)
            a_prev = jnp.exp(m_state - u_last)
            kw = k.astype(F32) * jnp.exp(b_col - u_last)
            c_ref[0, chain] = a_prev * c_state + _dot_tn(kw.astype(BF16), v)
            n_ref[0, chain] = a_prev * n_state + jnp.sum(kw, axis=0, keepdims=True)
            m_ref[0, chain] = jnp.broadcast_to(f_total + u_last, m_ref.shape[2:])


def _mlstm(q, k, v, gt, state, *, emit_h):
    b, t, width = q.shape
    chunk = min(MLSTM_CHUNK, t)
    nc = t // chunk
    hd = width // M_HEADS
    chains = 2 * M_HEADS
    fwd = pl.BlockSpec((1, chunk, width), lambda i, j: (i, j, 0))
    bwd = pl.BlockSpec((1, chunk, width), lambda i, j: (i, nc - 1 - j, 0))
    g_fwd = pl.BlockSpec((1, gt.shape[1], chunk), lambda i, j: (i, 0, j))
    g_bwd = pl.BlockSpec((1, gt.shape[1], chunk), lambda i, j: (i, 0, nc - 1 - j))
    c_spec = pl.BlockSpec((1, chains, hd, hd), lambda i, j: (i, 0, 0, 0))
    n_spec = pl.BlockSpec((1, chains, 1, hd), lambda i, j: (i, 0, 0, 0))
    m_spec = pl.BlockSpec((1, chains, 1, 128), lambda i, j: (i, 0, 0, 0))
    state_shapes = [jax.ShapeDtypeStruct((b, chains, hd, hd), F32),
                    jax.ShapeDtypeStruct((b, chains, 1, hd), F32),
                    jax.ShapeDtypeStruct((b, chains, 1, 128), F32)]
    h_shapes = [jax.ShapeDtypeStruct((b, t, width), BF16)] * 2 if emit_h else []
    h_specs = [fwd, bwd] if emit_h else []
    return pl.pallas_call(
        functools.partial(_mlstm_kernel, emit_h=emit_h),
        grid=(b, nc),
        in_specs=[fwd, fwd, fwd, g_fwd, bwd, bwd, bwd, g_bwd, c_spec, n_spec, m_spec],
        out_specs=h_specs + [c_spec, n_spec, m_spec],
        out_shape=h_shapes + state_shapes,
        compiler_params=_params("parallel", "arbitrary"),
        name="mlstm" if emit_h else "mlstm_ctx",
    )(q, k, v, gt, q, k, v, gt, *state)


def _stack_heads(q, hd):
    return jnp.concatenate([q[:, i * hd:(i + 1) * hd] for i in range(A_GROUPS)], axis=0)


def _store_heads(o_ref, out, tq, hd):
    for i in range(A_GROUPS):
        o_ref[0, :, i * hd:(i + 1) * hd] = out[i * tq:(i + 1) * tq, :].astype(BF16)


def _attn_bounded_kernel(bound_ref, q_ref, kl_ref, vl_ref, kc_ref, vc_ref, o_ref, *, k_rows):
    tq = q_ref.shape[1]
    hd = kl_ref.shape[2]
    qs = _stack_heads(q_ref[0], hd)
    bound = bound_ref[...]

    def chunk(k, v):
        return _dot(jnp.exp2(_dot_nt(qs, k) - bound).astype(BF16), v)

    def body(c, acc):
        start = pl.multiple_of(c * k_rows, k_rows)
        return acc + chunk(kl_ref[0, pl.ds(start, k_rows), :], vl_ref[0, pl.ds(start, k_rows), :])

    steps = kl_ref.shape[1] // k_rows
    acc = lax.fori_loop(0, steps, body, chunk(kc_ref[0], vc_ref[0]), unroll=min(ATTN_UNROLL, steps))
    _store_heads(o_ref, acc[:, :hd] / acc[:, hd:hd + 1], tq, hd)


def _attn_online_kernel(q_ref, kl_ref, vl_ref, kc_ref, vc_ref, o_ref, *, k_rows):
    tq = q_ref.shape[1]
    hd = kl_ref.shape[2]
    qs = _stack_heads(q_ref[0], hd)
    rows = qs.shape[0]

    def update(carry, k, v):
        m, acc = carry
        s = _dot_nt(qs, k)
        m_new = jnp.maximum(m, jnp.max(s, axis=-1, keepdims=True))
        acc = jnp.exp2(m - m_new) * acc + _dot(jnp.exp2(s - m_new).astype(BF16), v)
        return m_new, acc

    carry = (jnp.full((rows, 1), -jnp.inf, F32), jnp.zeros((rows, 2 * hd), F32))
    carry = update(carry, kc_ref[0], vc_ref[0])

    def body(c, carry):
        start = pl.multiple_of(c * k_rows, k_rows)
        return update(carry, kl_ref[0, pl.ds(start, k_rows), :], vl_ref[0, pl.ds(start, k_rows), :])

    _, acc = lax.fori_loop(0, kl_ref.shape[1] // k_rows, body, carry)
    _store_heads(o_ref, acc[:, :hd] / acc[:, hd:hd + 1], tq, hd)


def _attn(q, k_lat, v_lat, k_ctx, v_ctx, bound):
    b, s, width = q.shape
    hd = width // A_HEADS
    group = A_GROUPS * hd
    tq = min(ATTN_Q_ROWS, s)
    k_rows = min(ATTN_K_ROWS, s)
    ctx = k_ctx.shape[1]
    q_spec = pl.BlockSpec((1, tq, group), lambda i, g, j: (i, j, g))
    k_spec = lambda t: pl.BlockSpec((1, t, hd), lambda i, g, j: (i, 0, g))
    v_spec = lambda t: pl.BlockSpec((1, t, 2 * hd), lambda i, g, j: (i, 0, g))
    specs = [q_spec, k_spec(s), v_spec(s), k_spec(ctx), v_spec(ctx)]
    common = dict(
        grid=(b, A_KV_HEADS, s // tq),
        out_specs=q_spec,
        out_shape=jax.ShapeDtypeStruct((b, s, width), BF16),
        compiler_params=_params("parallel", "parallel", "arbitrary"),
    )

    def bounded(*args):
        return pl.pallas_call(functools.partial(_attn_bounded_kernel, k_rows=k_rows),
                              in_specs=[_resident((1, 1))] + specs, name="attn_bounded", **common)(*args)

    def online(_, *args):
        return pl.pallas_call(functools.partial(_attn_online_kernel, k_rows=k_rows),
                              in_specs=specs, name="attn_online", **common)(*args)

    safe = 2.0 * bound[0, 0] <= ATTN_SAFE_LOG2_SPAN
    return lax.cond(safe, bounded, online, bound, q, k_lat, v_lat, k_ctx, v_ctx)


def _tail_kernel(x_ref, hf_ref, hb_ref, og_ref, ya_ref, mg_ref, g1_ref, sc2_ref, sh2_ref, g2_ref,
                 gm_ref, n2_ref, fg_ref, w_pa, w_pb, w_o, w_g, w_u, w_d, o_ref):
    x = x_ref[0]
    d = x.shape[-1]
    hd = d // M_HEADS
    hsum = hf_ref[0].astype(F32) + hb_ref[0].astype(F32)
    hn = jnp.concatenate([_rms(hsum[:, i * hd:(i + 1) * hd], hd) for i in range(M_HEADS)], axis=-1)
    ym = (og_ref[0].astype(F32) * (hn * gm_ref[...])).astype(BF16)
    mg = mg_ref[0].astype(F32)
    merged = mg[:, :d] * _dot(ym, w_pa[...]) + mg[:, d:] * _dot(ya_ref[0], w_pb[...])
    x1 = x + g1_ref[0] * _dot(merged.astype(BF16), w_o[...])
    h2 = ((_rms(x1, d) * n2_ref[...]) * (1.0 + sc2_ref[0]) + sh2_ref[0]).astype(BF16)
    gate = _dot(h2, w_g[...])
    ff = _dot(((gate * _sigmoid(gate)) * _dot(h2, w_u[...])).astype(BF16), w_d[...])
    x2 = x1 + g2_ref[0] * ff
    o_ref[0] = _rms(x2, d) * fg_ref[...]


def _tail(x, hf, hb, og, ya, mg, g1, sc2, sh2, g2, gm, n2, fg, weights):
    b, t, d = x.shape
    tm = min(TAIL_ROWS, t)
    row = lambda n: pl.BlockSpec((1, tm, n), lambda i, j: (i, j, 0))
    per_b = pl.BlockSpec((1, 1, d), lambda i, j: (i, 0, 0))
    vec = _resident((1, d))
    return pl.pallas_call(
        _tail_kernel,
        grid=(b, t // tm),
        in_specs=[row(d), row(d), row(d), row(d), row(d), row(2 * d), per_b, per_b, per_b, per_b,
                  vec, vec, vec] + [_resident(w.shape) for w in weights],
        out_specs=row(d),
        out_shape=jax.ShapeDtypeStruct((b, t, d), F32),
        compiler_params=_params("parallel", "parallel"),
        name="tail",
    )(x, hf, hb, og, ya, mg, g1, sc2, sh2, g2, gm, n2, fg, *weights)


def _rope_tables(seq, head_dim):
    pairs = head_dim // 4
    pos = jnp.arange(seq, dtype=jnp.int32)
    row = (pos // GRID_W).astype(F32)
    col = (pos % GRID_W).astype(F32)
    inv = ROPE_THETA ** (-jnp.arange(pairs, dtype=F32) / pairs)
    ang = jnp.concatenate([row[:, None] * inv, col[:, None] * inv], axis=-1)
    cos, sin = jnp.cos(ang), jnp.sin(ang)
    return jnp.concatenate([cos, cos], axis=-1), jnp.concatenate([-sin, sin], axis=-1)


def _pair_split(head_dim):
    return jnp.concatenate([jnp.arange(0, head_dim, 2), jnp.arange(1, head_dim, 2)])


def kernel(x, c, ctx, c_ctx, w_mod, b_mod, norm1_g, norm2_g, w_in, gate_b, conv_w, conv_b, m_norm_g, q_norm_g,
           k_norm_g, w_pa, w_pb, w_o, w_ffn_gate, w_ffn_up, w_ffn_down, final_g):
    b, s, d = x.shape
    depth = w_mod.shape[0]
    assert depth == 1, "single trunk layer"
    m_hd = d // M_HEADS
    a_hd = d // A_HEADS
    kv_w = A_KV_HEADS * a_hd
    l = 0

    pad_rows = 8 * pl.cdiv(b + 1, 8) - (b + 1)
    c_rows = jnp.concatenate([c, c_ctx[None, :], jnp.zeros((pad_rows, d), F32)], axis=0)
    mod = _mod(c_rows, w_mod[l], b_mod[l])
    sh1, sc1, g1, sh2, sc2, g2 = [mod[:b, i * d:(i + 1) * d].reshape(b, 1, d) for i in range(6)]
    csh1, csc1 = [jnp.broadcast_to(mod[b, i * d:(i + 1) * d], (b, 1, d)) for i in range(2)]

    splits = (d, d, d, d, 4 * M_HEADS, d, kv_w, kv_w, 2 * d)
    offs = [0]
    for n in splits:
        offs.append(offs[-1] + n)
    cols = [w_in[l][:, offs[i]:offs[i + 1]] for i in range(len(splits))]
    perm = _pair_split(a_hd)
    perm_q = (jnp.arange(A_HEADS)[:, None] * a_hd + perm[None, :]).reshape(-1)
    perm_k = (jnp.arange(A_KV_HEADS)[:, None] * a_hd + perm[None, :]).reshape(-1)
    weights = [jnp.concatenate([cols[0], cols[1]], axis=1), cols[2], cols[3], cols[4],
               cols[5][:, perm_q], cols[6][:, perm_k], cols[7], cols[8]]
    weights = [w.astype(BF16) for w in weights]
    gq = (q_norm_g[l][perm] * (a_hd ** -0.5 * LOG2_E)).reshape(1, a_hd)
    gk = k_norm_g[l][perm].reshape(1, a_hd)
    score_bound = (a_hd * BF16_ROUNDING_MARGIN * jnp.max(jnp.abs(gq)) * jnp.max(jnp.abs(gk))).reshape(1, 1)
    gb = gate_b[l].reshape(-1, 1)
    cosf, sinf = _rope_tables(s, a_hd)
    n1 = norm1_g[l].reshape(1, d)

    lat = _inproj(x, sc1, sh1, n1, cosf, sinf, gq, gk, gb, weights, rope=True)
    cx = _inproj(ctx, csc1, csh1, n1, cosf, sinf, gq, gk, gb, weights, rope=False)
    qk_l, v_l, og_l, gt_l, aq_l, ak_l, av_l, mg_l = lat
    qk_c, v_c, _, gt_c, _, ak_c, av_c, _ = cx

    k_scale = m_hd ** -0.5
    q_c, k_c = _conv(qk_c, conv_w[l], conv_b[l], k_scale)
    q_l, k_l = _conv(qk_l, conv_w[l], conv_b[l], k_scale)
    chains = 2 * M_HEADS
    init = (jnp.zeros((b, chains, m_hd, m_hd), F32), jnp.zeros((b, chains, 1, m_hd), F32),
            jnp.full((b, chains, 1, 128), -jnp.inf, F32))
    ctx_state = _mlstm(q_c, k_c, v_c, gt_c, init, emit_h=False)
    hf, hb = _mlstm(q_l, k_l, v_l, gt_l, ctx_state, emit_h=True)[:2]

    ya = _attn(aq_l, ak_l, av_l, ak_c, av_c, score_bound)

    tail_w = [w.astype(BF16) for w in (w_pa[l], w_pb[l], w_o[l], w_ffn_gate[l], w_ffn_up[l], w_ffn_down[l])]
    return _tail(x, hf, hb, og_l, ya, mg_l, g1, sc2, sh2, g2, m_norm_g[l].reshape(1, d),
                 norm2_g[l].reshape(1, d), final_g.reshape(1, d), tail_w)
```

```python
import functools

import jax
import jax.numpy as jnp
from jax import lax
from jax.experimental import pallas as pl
from jax.experimental.pallas import tpu as pltpu

F32 = jnp.float32
BF16 = jnp.bfloat16

EPS = 1e-6
GRID_W = 64
ROPE_THETA = 10000.0
M_HEADS = 4
A_HEADS = 8
A_KV_HEADS = 2
A_GROUPS = A_HEADS // A_KV_HEADS
CONV_K = 5
CONV_PAD = CONV_K // 2
LOG2_E = 1.4426950408889634
BF16_ROUNDING_MARGIN = (1.0 + 2.0 ** -7) ** 2
ATTN_SAFE_LOG2_SPAN = 96.0

V7X_VMEM_BYTES = 64 * 1024 * 1024
VMEM_LIMIT = V7X_VMEM_BYTES - 8 * 1024 * 1024
F32_SUBLANES = 8
BF16_SUBLANES = 16
ATTN_SUM_ROWS = BF16_SUBLANES

INPROJ_ROWS = 256
MLSTM_CHUNK = 256
ATTN_Q_ROWS = 256
ATTN_K_ROWS = 512
TAIL_ROWS = 256


def _params(*sem):
    return pltpu.CompilerParams(dimension_semantics=sem, vmem_limit_bytes=VMEM_LIMIT)


def _resident(shape):
    zeros = (0,) * len(shape)
    return pl.BlockSpec(shape, lambda *_: zeros, pipeline_mode=pl.Buffered(1))


def _sigmoid(x):
    return 1.0 / (1.0 + jnp.exp(-x))


def _log_sigmoid(x):
    return jnp.minimum(x, 0.0) - jnp.log(1.0 + jnp.exp(-jnp.abs(x)))


def _rms(x, width):
    return x * lax.rsqrt(jnp.sum(x * x, axis=-1, keepdims=True) * (1.0 / width) + EPS)


def _dot(a, b):
    return jnp.dot(a, b, preferred_element_type=F32)


def _dot_nt(a, b):
    return lax.dot_general(a, b, (((1,), (1,)), ((), ())), preferred_element_type=F32)


def _dot_tn(a, b):
    return lax.dot_general(a, b, (((0,), (0,)), ((), ())), preferred_element_type=F32)


def _mod_kernel(c_ref, w_ref, b_ref, o_ref):
    c = c_ref[...]
    s = c * _sigmoid(c)
    o_ref[...] = jnp.dot(s, w_ref[...], preferred_element_type=F32, precision=lax.Precision.HIGHEST) + b_ref[...]


def _mod(c_rows, w_mod, b_mod):
    rows, d = c_rows.shape
    n = w_mod.shape[1]
    return pl.pallas_call(
        _mod_kernel,
        grid=(n // d,),
        in_specs=[
            pl.BlockSpec((rows, d), lambda j: (0, 0)),
            pl.BlockSpec((d, d), lambda j: (0, j)),
            pl.BlockSpec((1, d), lambda j: (0, j)),
        ],
        out_specs=pl.BlockSpec((rows, d), lambda j: (0, j)),
        out_shape=jax.ShapeDtypeStruct((rows, n), F32),
        compiler_params=_params("parallel"),
        name="mod",
    )(c_rows, w_mod, b_mod.reshape(1, n))


def _inproj_kernel(xp_ref, x_ref, xn_ref, sc_ref, sh_ref, g_ref, cos_ref, sin_ref, gq_ref, gk_ref, gb_ref,
                   cw_ref, cb_ref, w_mqk, w_mv, w_mo, w_gt, w_aq, w_ak, w_av, w_mg,
                   q_ref, kt_ref, v_ref, og_ref, gt_ref, aq_ref, ak_ref, av_ref, mg_ref, *, rope, k_scale):
    j = pl.program_id(1)
    last = pl.num_programs(1) - 1
    tm = x_ref.shape[1]
    halo = xp_ref.shape[1]
    d = x_ref.shape[-1]
    xs = jnp.concatenate([xp_ref[0], x_ref[0], xn_ref[0]], axis=0)
    h_ext = (_rms(xs, d) * g_ref[...]) * (1.0 + sc_ref[0]) + sh_ref[0]
    hb = h_ext[halo:halo + tm].astype(BF16)

    z = _dot(h_ext.astype(BF16), w_mqk[...])
    r = lax.broadcasted_iota(jnp.int32, (tm + 2 * halo, 1), 0)
    inside = ((r >= halo) | (j > 0)) & ((r < halo + tm) | (j < last))
    z = jnp.where(inside, z, 0.0)
    rows = tm + 2 * halo
    acc = cb_ref[...] + cw_ref[CONV_PAD:CONV_PAD + 1, :] * z[halo:halo + tm]
    for tap in range(CONV_K):
        if tap != CONV_PAD:
            shifted = pltpu.roll(z, (CONV_PAD - tap) % rows, axis=0)
            acc = acc + cw_ref[tap:tap + 1, :] * shifted[halo:halo + tm]
    y = acc * _sigmoid(acc)
    half = y.shape[-1] // 2
    q_ref[0] = y[:, :half].astype(BF16)
    kt_ref[0] = (y[:, half:] * k_scale).T.astype(BF16)

    v_ref[0] = _dot(hb, w_mv[...]).astype(BF16)
    og_ref[0] = _sigmoid(_dot(hb, w_mo[...])).astype(BF16)
    mg_ref[0] = _sigmoid(_dot(hb, w_mg[...])).astype(BF16)
    av = _dot(hb, w_av[...])
    hd = gk_ref.shape[-1]
    vt_rows = hd + ATTN_SUM_ROWS
    sum_rows = jnp.where(lax.broadcasted_iota(jnp.int32, (ATTN_SUM_ROWS, tm), 0) == 0, 1.0, 0.0).astype(BF16)
    for i in range(av.shape[-1] // hd):
        av_ref[0, i * vt_rows:i * vt_rows + hd, :] = av[:, i * hd:(i + 1) * hd].T.astype(BF16)
        av_ref[0, i * vt_rows + hd:(i + 1) * vt_rows, :] = sum_rows

    gt = _dot_nt(w_gt[...], hb) + gb_ref[...]
    row = lax.broadcasted_iota(jnp.int32, gt.shape, 0)
    is_forget = (row // M_HEADS) % 2 == 1
    gt_ref[0] = jnp.where(is_forget, _log_sigmoid(gt), gt)

    def heads(z, gain, out_ref):
        hd = gain.shape[-1]
        for i in range(z.shape[-1] // hd):
            zh = _rms(z[:, i * hd:(i + 1) * hd], hd) * gain
            if rope:
                zh = zh * cos_ref[...] + pltpu.roll(zh, hd // 2, axis=1) * sin_ref[...]
            out_ref[0, :, i * hd:(i + 1) * hd] = zh.astype(BF16)

    heads(_dot(hb, w_aq[...]), gq_ref[...], aq_ref)
    heads(_dot(hb, w_ak[...]), gk_ref[...], ak_ref)


def _inproj(x, scale, shift, gain, cosf, sinf, gq, gk, gate_b, conv_w, conv_b, weights, *, rope, k_scale):
    b, t, d = x.shape
    tm = min(INPROJ_ROWS, t)
    hd = gq.shape[-1]
    widths = [w.shape[1] for w in weights]
    n_mqk, n_mv, n_mo, n_gt, n_aq, n_ak, n_av, n_mg = widths
    w_gt_t = weights[3].T
    ws = list(weights)
    ws[3] = w_gt_t
    n_avt = (n_av // hd) * (hd + ATTN_SUM_ROWS)
    halo = F32_SUBLANES
    per = tm // halo
    nblk = t // halo
    row = lambda n: pl.BlockSpec((1, tm, n), lambda i, j: (i, j, 0))
    col = lambda n: pl.BlockSpec((1, n, tm), lambda i, j: (i, 0, j))
    prev_rows = pl.BlockSpec((1, halo, d), lambda i, j: (i, jnp.maximum(j * per - 1, 0), 0))
    next_rows = pl.BlockSpec((1, halo, d), lambda i, j: (i, jnp.minimum((j + 1) * per, nblk - 1), 0))
    per_b = pl.BlockSpec((1, 1, d), lambda i, j: (i, 0, 0))
    table = pl.BlockSpec((tm, hd), lambda i, j: (j, 0))
    out_shape = [
        jax.ShapeDtypeStruct((b, t, n_mqk // 2), BF16),
        jax.ShapeDtypeStruct((b, n_mqk // 2, t), BF16),
        jax.ShapeDtypeStruct((b, t, n_mv), BF16),
        jax.ShapeDtypeStruct((b, t, n_mo), BF16),
        jax.ShapeDtypeStruct((b, n_gt, t), F32),
        jax.ShapeDtypeStruct((b, t, n_aq), BF16),
        jax.ShapeDtypeStruct((b, t, n_ak), BF16),
        jax.ShapeDtypeStruct((b, n_avt, t), BF16),
        jax.ShapeDtypeStruct((b, t, n_mg), BF16),
    ]
    out_specs = [row(n_mqk // 2), col(n_mqk // 2), row(n_mv), row(n_mo), col(n_gt),
                 row(n_aq), row(n_ak), col(n_avt), row(n_mg)]
    return pl.pallas_call(
        functools.partial(_inproj_kernel, rope=rope, k_scale=k_scale),
        grid=(b, t // tm),
        in_specs=[prev_rows, row(d), next_rows, per_b, per_b, _resident((1, d)), table, table,
                  _resident((1, hd)), _resident((1, hd)), _resident((n_gt, 1)),
                  _resident(conv_w.shape), _resident((1, n_mqk))]
                 + [_resident(w.shape) for w in ws],
        out_specs=out_specs,
        out_shape=out_shape,
        compiler_params=_params("parallel", "parallel"),
        name="inproj_rope" if rope else "inproj_ctx",
    )(x, x, x, scale, shift, gain, cosf, sinf, gq, gk, gate_b, conv_w, conv_b.reshape(1, n_mqk), *ws)


def _mlstm_kernel(*refs, emit_h):
    (qf_ref, kf_ref, vf_ref, gf_ref, qb_ref, kb_ref, vb_ref, gb_ref, c0_ref, n0_ref, m0_ref) = refs[:11]
    if emit_h:
        hf_ref, hb_ref, c_ref, n_ref, m_ref = refs[11:]
    else:
        hf_ref = hb_ref = None
        c_ref, n_ref, m_ref = refs[11:]

    @pl.when(pl.program_id(1) == 0)
    def _():
        c_ref[...] = c0_ref[...]
        n_ref[...] = n0_ref[...]
        m_ref[...] = m0_ref[...]

    chunk = qf_ref.shape[1]
    hd = qf_ref.shape[2] // M_HEADS
    t_idx = lax.broadcasted_iota(jnp.int32, (chunk, chunk), 0)
    s_idx = lax.broadcasted_iota(jnp.int32, (chunk, chunk), 1)
    neg_inf = -jnp.inf
    ones = jnp.ones((chunk, 128), BF16)

    streams = ((qf_ref, kf_ref, vf_ref, gf_ref, hf_ref), (qb_ref, kb_ref, vb_ref, gb_ref, hb_ref))
    for direction, (q_ref, kt_ref, v_ref, g_ref, h_ref) in enumerate(streams):
        seen = s_idx <= t_idx if direction == 0 else s_idx >= t_idx
        gates = g_ref[0]
        upto = t_idx <= s_idx if direction == 0 else t_idx >= s_idx
        cum = jnp.dot(gates, jnp.where(upto, 1.0, 0.0), preferred_element_type=F32,
                      precision=lax.Precision.HIGHEST)
        for head in range(M_HEADS):
            chain = head * 2 + direction
            r_i = (2 * direction) * M_HEADS + head
            r_f = (2 * direction + 1) * M_HEADS + head
            lf_row = gates[r_f:r_f + 1, :]
            b_row = gates[r_i:r_i + 1, :] - cum[r_f:r_f + 1, :]
            f_total = jnp.sum(lf_row, axis=-1, keepdims=True)
            b_max = jnp.max(b_row, axis=-1, keepdims=True)

            lanes = slice(head * hd, (head + 1) * hd)
            q = q_ref[0, :, lanes]
            kt = kt_ref[0, lanes, :]
            v = v_ref[0, :, lanes]
            c_state = c_ref[0, chain]
            n_state = n_ref[0, chain]
            m_state = m_ref[0, chain][:, :1]

            if emit_h:
                a_col = jnp.sum(jnp.where(seen, lf_row, 0.0), axis=-1, keepdims=True)
                cmb_col = jnp.max(jnp.where(seen, b_row, neg_inf), axis=-1, keepdims=True)
                u_col = jnp.maximum(m_state, cmb_col)
                p = jnp.exp(jnp.where(seen, b_row - u_col, neg_inf))
                sp = (_dot(q, kt) * p).astype(BF16)
                w_inter = jnp.exp(m_state - u_col)
                num = w_inter * _dot(q, c_state.astype(BF16)) + _dot(sp, v)
                qn = _dot_nt(q, jnp.broadcast_to(n_state[:1], (128, hd)).astype(BF16))
                den = w_inter * qn + _dot(sp, ones)
                inv = 1.0 / jnp.maximum(jnp.abs(den), jnp.exp(-(a_col + u_col)))
                h_ref[0, :, lanes] = (num * jnp.concatenate([inv] * (hd // 128), axis=-1)).astype(BF16)

            u_last = jnp.maximum(m_state, b_max)
            a_prev = jnp.exp(m_state - u_last)
            w_row = jnp.exp(b_row - u_last)
            ktw = (kt.astype(F32) * w_row).astype(BF16)
            c_ref[0, chain] = a_prev * c_state + _dot(ktw, v)
            n_ref[0, chain] = a_prev * n_state + _dot_nt(jnp.broadcast_to(w_row, (8, chunk)).astype(BF16), kt)
            m_ref[0, chain] = jnp.broadcast_to(f_total + u_last, m_ref.shape[2:])


def _mlstm(q, kt, v, gt, state, *, emit_h):
    b, t, width = q.shape
    chunk = min(MLSTM_CHUNK, t)
    nc = t // chunk
    hd = width // M_HEADS
    chains = 2 * M_HEADS
    fwd = pl.BlockSpec((1, chunk, width), lambda i, j: (i, j, 0))
    bwd = pl.BlockSpec((1, chunk, width), lambda i, j: (i, nc - 1 - j, 0))
    col_fwd = lambda n: pl.BlockSpec((1, n, chunk), lambda i, j: (i, 0, j))
    col_bwd = lambda n: pl.BlockSpec((1, n, chunk), lambda i, j: (i, 0, nc - 1 - j))
    c_spec = pl.BlockSpec((1, chains, hd, hd), lambda i, j: (i, 0, 0, 0))
    n_spec = pl.BlockSpec((1, chains, 8, hd), lambda i, j: (i, 0, 0, 0))
    m_spec = pl.BlockSpec((1, chains, 1, 128), lambda i, j: (i, 0, 0, 0))
    state_shapes = [jax.ShapeDtypeStruct((b, chains, hd, hd), F32),
                    jax.ShapeDtypeStruct((b, chains, 8, hd), F32),
                    jax.ShapeDtypeStruct((b, chains, 1, 128), F32)]
    h_shapes = [jax.ShapeDtypeStruct((b, t, width), BF16)] * 2 if emit_h else []
    h_specs = [fwd, bwd] if emit_h else []
    n_g = gt.shape[1]
    return pl.pallas_call(
        functools.partial(_mlstm_kernel, emit_h=emit_h),
        grid=(b, nc),
        in_specs=[fwd, col_fwd(width), fwd, col_fwd(n_g), bwd, col_bwd(width), bwd, col_bwd(n_g),
                  c_spec, n_spec, m_spec],
        out_specs=h_specs + [c_spec, n_spec, m_spec],
        out_shape=h_shapes + state_shapes,
        compiler_params=_params("parallel", "arbitrary"),
        name="mlstm" if emit_h else "mlstm_ctx",
    )(q, kt, v, gt, q, kt, v, gt, *state)


def _stack_heads(q, hd):
    return jnp.concatenate([q[:, i * hd:(i + 1) * hd] for i in range(A_GROUPS)], axis=0)


def _store_heads(o_ref, acc_t, tq, hd):
    out = (acc_t[:hd] / acc_t[hd:hd + 1]).T
    for i in range(A_GROUPS):
        o_ref[0, :, i * hd:(i + 1) * hd] = out[i * tq:(i + 1) * tq, :].astype(BF16)


def _key_chunks(kl_ref, vl_ref, kc_ref, vc_ref, k_rows):
    yield kc_ref[0], vc_ref[0]
    for start in range(0, kl_ref.shape[1], k_rows):
        yield kl_ref[0, start:start + k_rows, :], vl_ref[0, :, start:start + k_rows]


def _attn_bounded_kernel(bound_ref, q_ref, kl_ref, vl_ref, kc_ref, vc_ref, o_ref, *, k_rows):
    tq = q_ref.shape[1]
    hd = kl_ref.shape[2]
    qs = _stack_heads(q_ref[0], hd)
    bound = bound_ref[...]
    acc = None
    for k, vt in _key_chunks(kl_ref, vl_ref, kc_ref, vc_ref, k_rows):
        part = _dot(vt, jnp.exp2(_dot_nt(k, qs) - bound).astype(BF16))
        acc = part if acc is None else acc + part
    _store_heads(o_ref, acc, tq, hd)


def _attn_online_kernel(q_ref, kl_ref, vl_ref, kc_ref, vc_ref, o_ref, *, k_rows):
    tq = q_ref.shape[1]
    hd = kl_ref.shape[2]
    qs = _stack_heads(q_ref[0], hd)
    m = jnp.full((1, qs.shape[0]), -jnp.inf, F32)
    acc = jnp.zeros((vc_ref.shape[1], qs.shape[0]), F32)
    for k, vt in _key_chunks(kl_ref, vl_ref, kc_ref, vc_ref, k_rows):
        s = _dot_nt(k, qs)
        m_new = jnp.maximum(m, jnp.max(s, axis=0, keepdims=True))
        acc = jnp.exp2(m - m_new) * acc + _dot(vt, jnp.exp2(s - m_new).astype(BF16))
        m = m_new
    _store_heads(o_ref, acc, tq, hd)


def _attn(q, k_lat, v_lat, k_ctx, v_ctx, bound):
    b, s, width = q.shape
    hd = width // A_HEADS
    group = A_GROUPS * hd
    tq = min(ATTN_Q_ROWS, s)
    k_rows = min(ATTN_K_ROWS, s)
    ctx = k_ctx.shape[1]
    q_spec = pl.BlockSpec((1, tq, group), lambda i, g, j: (i, j, g))
    k_spec = lambda t: pl.BlockSpec((1, t, hd), lambda i, g, j: (i, 0, g))
    v_spec = lambda t: pl.BlockSpec((1, hd + ATTN_SUM_ROWS, t), lambda i, g, j: (i, g, 0))
    specs = [q_spec, k_spec(s), v_spec(s), k_spec(ctx), v_spec(ctx)]
    common = dict(
        grid=(b, A_KV_HEADS, s // tq),
        out_specs=q_spec,
        out_shape=jax.ShapeDtypeStruct((b, s, width), BF16),
        compiler_params=_params("parallel", "parallel", "arbitrary"),
    )

    def bounded(*args):
        return pl.pallas_call(functools.partial(_attn_bounded_kernel, k_rows=k_rows),
                              in_specs=[_resident((1, 1))] + specs, name="attn_bounded", **common)(*args)

    def online(_, *args):
        return pl.pallas_call(functools.partial(_attn_online_kernel, k_rows=k_rows),
                              in_specs=specs, name="attn_online", **common)(*args)

    safe = 2.0 * bound[0, 0] <= ATTN_SAFE_LOG2_SPAN
    return lax.cond(safe, bounded, online, bound, q, k_lat, v_lat, k_ctx, v_ctx)


def _tail_kernel(x_ref, hf_ref, hb_ref, og_ref, ya_ref, mg_ref, g1_ref, sc2_ref, sh2_ref, g2_ref,
                 gm_ref, n2_ref, fg_ref, w_pa, w_pb, w_o, w_g, w_u, w_d, o_ref):
    x = x_ref[0]
    d = x.shape[-1]
    hd = d // M_HEADS
    hsum = hf_ref[0].astype(F32) + hb_ref[0].astype(F32)
    hn = jnp.concatenate([_rms(hsum[:, i * hd:(i + 1) * hd], hd) for i in range(M_HEADS)], axis=-1)
    ym = (og_ref[0].astype(F32) * (hn * gm_ref[...])).astype(BF16)
    mg = mg_ref[0].astype(F32)
    merged = mg[:, :d] * _dot(ym, w_pa[...]) + mg[:, d:] * _dot(ya_ref[0], w_pb[...])
    x1 = x + g1_ref[0] * _dot(merged.astype(BF16), w_o[...])
    h2 = ((_rms(x1, d) * n2_ref[...]) * (1.0 + sc2_ref[0]) + sh2_ref[0]).astype(BF16)
    gate = _dot(h2, w_g[...])
    ff = _dot(((gate * _sigmoid(gate)) * _dot(h2, w_u[...])).astype(BF16), w_d[...])
    x2 = x1 + g2_ref[0] * ff
    o_ref[0] = _rms(x2, d) * fg_ref[...]


def _tail(x, hf, hb, og, ya, mg, g1, sc2, sh2, g2, gm, n2, fg, weights):
    b, t, d = x.shape
    tm = min(TAIL_ROWS, t)
    row = lambda n: pl.BlockSpec((1, tm, n), lambda i, j: (i, j, 0))
    per_b = pl.BlockSpec((1, 1, d), lambda i, j: (i, 0, 0))
    vec = _resident((1, d))
    return pl.pallas_call(
        _tail_kernel,
        grid=(b, t // tm),
        in_specs=[row(d), row(d), row(d), row(d), row(d), row(2 * d), per_b, per_b, per_b, per_b,
                  vec, vec, vec] + [_resident(w.shape) for w in weights],
        out_specs=row(d),
        out_shape=jax.ShapeDtypeStruct((b, t, d), F32),
        compiler_params=_params("parallel", "parallel"),
        name="tail",
    )(x, hf, hb, og, ya, mg, g1, sc2, sh2, g2, gm, n2, fg, *weights)


def _rope_tables(seq, head_dim):
    pairs = head_dim // 4
    pos = jnp.arange(seq, dtype=jnp.int32)
    row = (pos // GRID_W).astype(F32)
    col = (pos % GRID_W).astype(F32)
    inv = ROPE_THETA ** (-jnp.arange(pairs, dtype=F32) / pairs)
    ang = jnp.concatenate([row[:, None] * inv, col[:, None] * inv], axis=-1)
    cos, sin = jnp.cos(ang), jnp.sin(ang)
    return jnp.concatenate([cos, cos], axis=-1), jnp.concatenate([-sin, sin], axis=-1)


def _pair_split(head_dim):
    return jnp.concatenate([jnp.arange(0, head_dim, 2), jnp.arange(1, head_dim, 2)])


def kernel(x, c, ctx, c_ctx, w_mod, b_mod, norm1_g, norm2_g, w_in, gate_b, conv_w, conv_b, m_norm_g, q_norm_g,
           k_norm_g, w_pa, w_pb, w_o, w_ffn_gate, w_ffn_up, w_ffn_down, final_g):
    b, s, d = x.shape
    depth = w_mod.shape[0]
    assert depth == 1, "single trunk layer"
    m_hd = d // M_HEADS
    a_hd = d // A_HEADS
    kv_w = A_KV_HEADS * a_hd
    l = 0

    pad_rows = 8 * pl.cdiv(b + 1, 8) - (b + 1)
    c_rows = jnp.concatenate([c, c_ctx[None, :], jnp.zeros((pad_rows, d), F32)], axis=0)
    mod = _mod(c_rows, w_mod[l], b_mod[l])
    sh1, sc1, g1, sh2, sc2, g2 = [mod[:b, i * d:(i + 1) * d].reshape(b, 1, d) for i in range(6)]
    csh1, csc1 = [jnp.broadcast_to(mod[b, i * d:(i + 1) * d], (b, 1, d)) for i in range(2)]

    splits = (d, d, d, d, 4 * M_HEADS, d, kv_w, kv_w, 2 * d)
    offs = [0]
    for n in splits:
        offs.append(offs[-1] + n)
    cols = [w_in[l][:, offs[i]:offs[i + 1]] for i in range(len(splits))]
    perm = _pair_split(a_hd)
    perm_q = (jnp.arange(A_HEADS)[:, None] * a_hd + perm[None, :]).reshape(-1)
    perm_k = (jnp.arange(A_KV_HEADS)[:, None] * a_hd + perm[None, :]).reshape(-1)
    weights = [jnp.concatenate([cols[0], cols[1]], axis=1), cols[2], cols[3], cols[4],
               cols[5][:, perm_q], cols[6][:, perm_k], cols[7], cols[8]]
    weights = [w.astype(BF16) for w in weights]
    gq = (q_norm_g[l][perm] * (a_hd ** -0.5 * LOG2_E)).reshape(1, a_hd)
    gk = k_norm_g[l][perm].reshape(1, a_hd)
    score_bound = (a_hd * BF16_ROUNDING_MARGIN * jnp.max(jnp.abs(gq)) * jnp.max(jnp.abs(gk))).reshape(1, 1)
    gb = gate_b[l].reshape(-1, 1)
    cosf, sinf = _rope_tables(s, a_hd)
    n1 = norm1_g[l].reshape(1, d)

    proj = functools.partial(_inproj, gain=n1, cosf=cosf, sinf=sinf, gq=gq, gk=gk, gate_b=gb, conv_w=conv_w[l],
                             conv_b=conv_b[l], weights=weights, k_scale=m_hd ** -0.5)
    q_l, kt_l, v_l, og_l, gt_l, aq_l, ak_l, av_l, mg_l = proj(x, sc1, sh1, rope=True)
    q_c, kt_c, v_c, _, gt_c, _, ak_c, av_c, _ = proj(ctx, csc1, csh1, rope=False)

    chains = 2 * M_HEADS
    init = (jnp.zeros((b, chains, m_hd, m_hd), F32), jnp.zeros((b, chains, 8, m_hd), F32),
            jnp.full((b, chains, 1, 128), -jnp.inf, F32))
    ctx_state = _mlstm(q_c, kt_c, v_c, gt_c, init, emit_h=False)
    hf, hb = _mlstm(q_l, kt_l, v_l, gt_l, ctx_state, emit_h=True)[:2]

    ya = _attn(aq_l, ak_l, av_l, ak_c, av_c, score_bound)

    tail_w = [w.astype(BF16) for w in (w_pa[l], w_pb[l], w_o[l], w_ffn_gate[l], w_ffn_up[l], w_ffn_down[l])]
    return _tail(x, hf, hb, og_l, ya, mg_l, g1, sc2, sh2, g2, m_norm_g[l].reshape(1, d),
                 norm2_g[l].reshape(1, d), final_g.reshape(1, d), tail_w)
```

```python
import functools

import jax
import jax.numpy as jnp
from jax import lax
from jax.experimental import pallas as pl
from jax.experimental.pallas import tpu as pltpu

F32 = jnp.float32
BF16 = jnp.bfloat16

EPS = 1e-6
GRID_W = 64
ROPE_THETA = 10000.0
M_HEADS = 4
A_HEADS = 8
A_KV_HEADS = 2
A_GROUPS = A_HEADS // A_KV_HEADS
CONV_K = 5
CONV_PAD = CONV_K // 2
LOG2_E = 1.4426950408889634
BF16_ROUNDING_MARGIN = (1.0 + 2.0 ** -7) ** 2
ATTN_SAFE_LOG2_SPAN = 96.0

V7X_VMEM_BYTES = 64 * 1024 * 1024
VMEM_LIMIT = V7X_VMEM_BYTES - 8 * 1024 * 1024
F32_SUBLANES = 8
BF16_SUBLANES = 16
ATTN_SUM_ROWS = BF16_SUBLANES

INPROJ_ROWS = 256
ATTN_Q_ROWS = 512
ATTN_K_ROWS = 1024
TAIL_ROWS = 256


def _params(*sem):
    return pltpu.CompilerParams(dimension_semantics=sem, vmem_limit_bytes=VMEM_LIMIT)


def _resident(shape):
    zeros = (0,) * len(shape)
    return pl.BlockSpec(shape, lambda *_: zeros, pipeline_mode=pl.Buffered(1))


def _sigmoid(x):
    return 1.0 / (1.0 + jnp.exp(-x))


def _log_sigmoid(x):
    return jnp.minimum(x, 0.0) - jnp.log(1.0 + jnp.exp(-jnp.abs(x)))


def _rms(x, width):
    return x * lax.rsqrt(jnp.sum(x * x, axis=-1, keepdims=True) * (1.0 / width) + EPS)


def _dot(a, b):
    return jnp.dot(a, b, preferred_element_type=F32)


def _dot_nt(a, b):
    return lax.dot_general(a, b, (((1,), (1,)), ((), ())), preferred_element_type=F32)


def _dot_tn(a, b):
    return lax.dot_general(a, b, (((0,), (0,)), ((), ())), preferred_element_type=F32)


def _mod_kernel(c_ref, w_ref, b_ref, o_ref):
    c = c_ref[...]
    s = c * _sigmoid(c)
    o_ref[...] = jnp.dot(s, w_ref[...], preferred_element_type=F32, precision=lax.Precision.HIGHEST) + b_ref[...]


def _mod(c_rows, w_mod, b_mod):
    rows, d = c_rows.shape
    n = w_mod.shape[1]
    return pl.pallas_call(
        _mod_kernel,
        grid=(n // d,),
        in_specs=[
            pl.BlockSpec((rows, d), lambda j: (0, 0)),
            pl.BlockSpec((d, d), lambda j: (0, j)),
            pl.BlockSpec((1, d), lambda j: (0, j)),
        ],
        out_specs=pl.BlockSpec((rows, d), lambda j: (0, j)),
        out_shape=jax.ShapeDtypeStruct((rows, n), F32),
        compiler_params=_params("parallel"),
        name="mod",
    )(c_rows, w_mod, b_mod.reshape(1, n))


def _inproj_kernel(xp_ref, x_ref, xn_ref, sc_ref, sh_ref, g_ref, cos_ref, sin_ref, gq_ref, gk_ref, gb_ref,
                   cw_ref, cb_ref, w_mqk, w_gt, w_wide,
                   q_ref, kt_ref, v_ref, og_ref, gt_ref, aq_ref, ak_ref, av_ref, mg_ref, *, rope, k_scale):
    j = pl.program_id(1)
    last = pl.num_programs(1) - 1
    tm = x_ref.shape[1]
    halo = xp_ref.shape[1]
    d = x_ref.shape[-1]
    xs = jnp.concatenate([xp_ref[0], x_ref[0], xn_ref[0]], axis=0)
    h_ext = (_rms(xs, d) * g_ref[...]) * (1.0 + sc_ref[0]) + sh_ref[0]
    hb = h_ext[halo:halo + tm].astype(BF16)

    z = _dot(h_ext.astype(BF16), w_mqk[...])
    r = lax.broadcasted_iota(jnp.int32, (tm + 2 * halo, 1), 0)
    inside = ((r >= halo) | (j > 0)) & ((r < halo + tm) | (j < last))
    z = jnp.where(inside, z, 0.0)
    rows = tm + 2 * halo
    acc = cb_ref[...] + cw_ref[CONV_PAD:CONV_PAD + 1, :] * z[halo:halo + tm]
    for tap in range(CONV_K):
        if tap != CONV_PAD:
            shifted = pltpu.roll(z, (CONV_PAD - tap) % rows, axis=0)
            acc = acc + cw_ref[tap:tap + 1, :] * shifted[halo:halo + tm]
    y = acc * _sigmoid(acc)
    half = y.shape[-1] // 2
    q_ref[0] = y[:, :half].astype(BF16)
    kt_ref[0, 0] = (y[:, half:] * k_scale).astype(BF16).T

    def heads(z, gain, out_ref):
        hd = gain.shape[-1]
        for i in range(z.shape[-1] // hd):
            zh = _rms(z[:, i * hd:(i + 1) * hd], hd) * gain
            if rope:
                zh = zh * cos_ref[...] + pltpu.roll(zh, hd // 2, axis=1) * sin_ref[...]
            out_ref[0, :, i * hd:(i + 1) * hd] = zh.astype(BF16)

    wide = _dot(hb, w_wide[...])
    edges = [0]
    for ref in (aq_ref, ak_ref, v_ref, og_ref, mg_ref):
        edges.append(edges[-1] + ref.shape[-1])
    z_aq, z_ak, z_v, z_og, z_mg = [wide[:, lo:hi] for lo, hi in zip(edges[:-1], edges[1:])]
    heads(z_aq, gq_ref[...], aq_ref)
    heads(z_ak, gk_ref[...], ak_ref)
    v_ref[0] = z_v.astype(BF16)
    og_ref[0] = _sigmoid(z_og).astype(BF16)
    mg_ref[0] = _sigmoid(z_mg).astype(BF16)
    av = wide[:, edges[-1]:].astype(BF16)
    hd = gk_ref.shape[-1]
    vt_rows = hd + ATTN_SUM_ROWS
    sum_rows = jnp.where(lax.broadcasted_iota(jnp.int32, (ATTN_SUM_ROWS, tm), 0) == 0, 1.0, 0.0).astype(BF16)
    for i in range(av.shape[-1] // hd):
        av_ref[0, 0, i * vt_rows:i * vt_rows + hd, :] = av[:, i * hd:(i + 1) * hd].T
        av_ref[0, 0, i * vt_rows + hd:(i + 1) * vt_rows, :] = sum_rows

    gt = _dot_nt(w_gt[...], hb) + gb_ref[...]
    row = lax.broadcasted_iota(jnp.int32, gt.shape, 0)
    is_forget = (row // M_HEADS) % 2 == 1
    gt_ref[0, 0] = jnp.where(is_forget, _log_sigmoid(gt), gt)


def _inproj(x, scale, shift, gain, cosf, sinf, gq, gk, gate_b, conv_w, conv_b, weights, *, rope, k_scale):
    b, t, d = x.shape
    tm = min(INPROJ_ROWS, t)
    hd = gq.shape[-1]
    widths = [w.shape[1] for w in weights]
    n_mqk, n_mv, n_mo, n_gt, n_aq, n_ak, n_av, n_mg = widths
    w_mqk, w_mv, w_mo, w_gt, w_aq, w_ak, w_av, w_mg = weights
    ws = [w_mqk, w_gt.T, jnp.concatenate([w_aq, w_ak, w_mv, w_mo, w_mg, w_av], axis=1)]
    n_avt = (n_av // hd) * (hd + ATTN_SUM_ROWS)
    halo = F32_SUBLANES
    per = tm // halo
    nblk = t // halo
    nt = t // tm
    row = lambda n: pl.BlockSpec((1, tm, n), lambda i, j: (i, j, 0))
    col = lambda n: pl.BlockSpec((1, 1, n, tm), lambda i, j: (i, j, 0, 0))
    prev_rows = pl.BlockSpec((1, halo, d), lambda i, j: (i, jnp.maximum(j * per - 1, 0), 0))
    next_rows = pl.BlockSpec((1, halo, d), lambda i, j: (i, jnp.minimum((j + 1) * per, nblk - 1), 0))
    per_b = pl.BlockSpec((1, 1, d), lambda i, j: (i, 0, 0))
    table = pl.BlockSpec((tm, hd), lambda i, j: (j, 0))
    out_shape = [
        jax.ShapeDtypeStruct((b, t, n_mqk // 2), BF16),
        jax.ShapeDtypeStruct((b, nt, n_mqk // 2, tm), BF16),
        jax.ShapeDtypeStruct((b, t, n_mv), BF16),
        jax.ShapeDtypeStruct((b, t, n_mo), BF16),
        jax.ShapeDtypeStruct((b, nt, n_gt, tm), F32),
        jax.ShapeDtypeStruct((b, t, n_aq), BF16),
        jax.ShapeDtypeStruct((b, t, n_ak), BF16),
        jax.ShapeDtypeStruct((b, nt, n_avt, tm), BF16),
        jax.ShapeDtypeStruct((b, t, n_mg), BF16),
    ]
    out_specs = [row(n_mqk // 2), col(n_mqk // 2), row(n_mv), row(n_mo), col(n_gt),
                 row(n_aq), row(n_ak), col(n_avt), row(n_mg)]
    return pl.pallas_call(
        functools.partial(_inproj_kernel, rope=rope, k_scale=k_scale),
        grid=(b, t // tm),
        in_specs=[prev_rows, row(d), next_rows, per_b, per_b, _resident((1, d)), table, table,
                  _resident((1, hd)), _resident((1, hd)), _resident((n_gt, 1)),
                  _resident(conv_w.shape), _resident((1, n_mqk))]
                 + [_resident(w.shape) for w in ws],
        out_specs=out_specs,
        out_shape=out_shape,
        compiler_params=_params("parallel", "parallel"),
        name="inproj_rope" if rope else "inproj_ctx",
    )(x, x, x, scale, shift, gain, cosf, sinf, gq, gk, gate_b, conv_w, conv_b.reshape(1, n_mqk), *ws)


def _mlstm_kernel(*refs, emit_h):
    (qf_ref, kf_ref, vf_ref, gf_ref, qb_ref, kb_ref, vb_ref, gb_ref, c0_ref, n0_ref, m0_ref) = refs[:11]
    if emit_h:
        hf_ref, hb_ref, c_ref, n_ref, m_ref = refs[11:]
    else:
        hf_ref = hb_ref = None
        c_ref, n_ref, m_ref = refs[11:]

    @pl.when(pl.program_id(1) == 0)
    def _():
        c_ref[...] = c0_ref[...]
        n_ref[...] = n0_ref[...]
        m_ref[...] = m0_ref[...]

    chunk = qf_ref.shape[1]
    hd = qf_ref.shape[2] // M_HEADS
    t_idx = lax.broadcasted_iota(jnp.int32, (chunk, chunk), 0)
    s_idx = lax.broadcasted_iota(jnp.int32, (chunk, chunk), 1)
    neg_inf = -jnp.inf
    ones = jnp.ones((chunk, 128), BF16)

    streams = ((qf_ref, kf_ref, vf_ref, gf_ref, hf_ref), (qb_ref, kb_ref, vb_ref, gb_ref, hb_ref))
    for direction, (q_ref, kt_ref, v_ref, g_ref, h_ref) in enumerate(streams):
        seen = s_idx <= t_idx if direction == 0 else s_idx >= t_idx
        gates = g_ref[0, 0]
        upto = t_idx <= s_idx if direction == 0 else t_idx >= s_idx
        cum = jnp.dot(gates, jnp.where(upto, 1.0, 0.0), preferred_element_type=F32,
                      precision=lax.Precision.HIGHEST)
        for head in range(M_HEADS):
            chain = head * 2 + direction
            r_i = (2 * direction) * M_HEADS + head
            r_f = (2 * direction + 1) * M_HEADS + head
            lf_row = gates[r_f:r_f + 1, :]
            b_row = gates[r_i:r_i + 1, :] - cum[r_f:r_f + 1, :]
            f_total = jnp.sum(lf_row, axis=-1, keepdims=True)
            b_max = jnp.max(b_row, axis=-1, keepdims=True)

            lanes = slice(head * hd, (head + 1) * hd)
            q = q_ref[0, :, lanes]
            kt = kt_ref[0, 0, lanes, :]
            v = v_ref[0, :, lanes]
            c_state = c_ref[0, chain]
            n_state = n_ref[0, chain]
            m_state = m_ref[0, chain][:, :1]

            if emit_h:
                a_col = jnp.sum(jnp.where(seen, lf_row, 0.0), axis=-1, keepdims=True)
                cmb_col = jnp.max(jnp.where(seen, b_row, neg_inf), axis=-1, keepdims=True)
                u_col = jnp.maximum(m_state, cmb_col)
                p = jnp.exp(jnp.where(seen, b_row - u_col, neg_inf))
                sp = (_dot(q, kt) * p).astype(BF16)
                w_inter = jnp.exp(m_state - u_col)
                num = w_inter * _dot(q, c_state.astype(BF16)) + _dot(sp, v)
                qn = _dot_nt(q, jnp.broadcast_to(n_state[:1], (128, hd)).astype(BF16))
                den = w_inter * qn + _dot(sp, ones)
                inv = 1.0 / jnp.maximum(jnp.abs(den), jnp.exp(-(a_col + u_col)))
                h_ref[0, :, lanes] = (num * jnp.concatenate([inv] * (hd // 128), axis=-1)).astype(BF16)

            u_last = jnp.maximum(m_state, b_max)
            a_prev = jnp.exp(m_state - u_last)
            w_row = jnp.exp(b_row - u_last)
            ktw = (kt.astype(F32) * w_row).astype(BF16)
            c_ref[0, chain] = a_prev * c_state + _dot(ktw, v)
            n_ref[0, chain] = a_prev * n_state + _dot_nt(jnp.broadcast_to(w_row, (8, chunk)).astype(BF16), kt)
            m_ref[0, chain] = jnp.broadcast_to(f_total + u_last, m_ref.shape[2:])


def _mlstm(q, kt, v, gt, state, *, emit_h):
    b, t, width = q.shape
    nc, chunk = kt.shape[1], kt.shape[3]
    assert nc * chunk == t and gt.shape[1] == nc
    hd = width // M_HEADS
    chains = 2 * M_HEADS
    fwd = pl.BlockSpec((1, chunk, width), lambda i, j: (i, j, 0))
    bwd = pl.BlockSpec((1, chunk, width), lambda i, j: (i, nc - 1 - j, 0))
    col_fwd = lambda n: pl.BlockSpec((1, 1, n, chunk), lambda i, j: (i, j, 0, 0))
    col_bwd = lambda n: pl.BlockSpec((1, 1, n, chunk), lambda i, j: (i, nc - 1 - j, 0, 0))
    c_spec = pl.BlockSpec((1, chains, hd, hd), lambda i, j: (i, 0, 0, 0))
    n_spec = pl.BlockSpec((1, chains, 8, hd), lambda i, j: (i, 0, 0, 0))
    m_spec = pl.BlockSpec((1, chains, 1, 128), lambda i, j: (i, 0, 0, 0))
    state_shapes = [jax.ShapeDtypeStruct((b, chains, hd, hd), F32),
                    jax.ShapeDtypeStruct((b, chains, 8, hd), F32),
                    jax.ShapeDtypeStruct((b, chains, 1, 128), F32)]
    h_shapes = [jax.ShapeDtypeStruct((b, t, width), BF16)] * 2 if emit_h else []
    h_specs = [fwd, bwd] if emit_h else []
    n_g = gt.shape[2]
    return pl.pallas_call(
        functools.partial(_mlstm_kernel, emit_h=emit_h),
        grid=(b, nc),
        in_specs=[fwd, col_fwd(width), fwd, col_fwd(n_g), bwd, col_bwd(width), bwd, col_bwd(n_g),
                  c_spec, n_spec, m_spec],
        out_specs=h_specs + [c_spec, n_spec, m_spec],
        out_shape=h_shapes + state_shapes,
        compiler_params=_params("parallel", "arbitrary"),
        name="mlstm" if emit_h else "mlstm_ctx",
    )(q, kt, v, gt, q, kt, v, gt, *state)


def _stack_heads(q, hd):
    return jnp.concatenate([q[:, i * hd:(i + 1) * hd] for i in range(A_GROUPS)], axis=0)


def _store_heads(o_ref, acc_t, tq, hd):
    out = (acc_t[:hd] / acc_t[hd:hd + 1]).T
    for i in range(A_GROUPS):
        o_ref[0, :, i * hd:(i + 1) * hd] = out[i * tq:(i + 1) * tq, :].astype(BF16)


def _key_chunks(kl_ref, vl_ref, kc_ref, vc_ref, k_rows):
    def vt(ref, first, count):
        return jnp.concatenate([ref[0, first + i] for i in range(count)], axis=-1)

    tile = vl_ref.shape[3]
    yield kc_ref[0], vt(vc_ref, 0, vc_ref.shape[1])
    for start in range(0, kl_ref.shape[1], k_rows):
        yield kl_ref[0, start:start + k_rows, :], vt(vl_ref, start // tile, k_rows // tile)


def _attn_bounded_kernel(bound_ref, q_ref, kl_ref, vl_ref, kc_ref, vc_ref, o_ref, *, k_rows):
    tq = q_ref.shape[1]
    hd = kl_ref.shape[2]
    qs = _stack_heads(q_ref[0], hd)
    bound = bound_ref[...]
    acc = None
    for k, vt in _key_chunks(kl_ref, vl_ref, kc_ref, vc_ref, k_rows):
        part = _dot(vt, jnp.exp2(_dot_nt(k, qs) - bound).astype(BF16))
        acc = part if acc is None else acc + part
    _store_heads(o_ref, acc, tq, hd)


def _attn_online_kernel(q_ref, kl_ref, vl_ref, kc_ref, vc_ref, o_ref, *, k_rows):
    tq = q_ref.shape[1]
    hd = kl_ref.shape[2]
    qs = _stack_heads(q_ref[0], hd)
    m = jnp.full((1, qs.shape[0]), -jnp.inf, F32)
    acc = jnp.zeros((vc_ref.shape[2], qs.shape[0]), F32)
    for k, vt in _key_chunks(kl_ref, vl_ref, kc_ref, vc_ref, k_rows):
        s = _dot_nt(k, qs)
        m_new = jnp.maximum(m, jnp.max(s, axis=0, keepdims=True))
        acc = jnp.exp2(m - m_new) * acc + _dot(vt, jnp.exp2(s - m_new).astype(BF16))
        m = m_new
    _store_heads(o_ref, acc, tq, hd)


def _attn(q, k_lat, v_lat, k_ctx, v_ctx, bound):
    b, s, width = q.shape
    hd = width // A_HEADS
    group = A_GROUPS * hd
    tq = min(ATTN_Q_ROWS, s)
    k_rows = min(ATTN_K_ROWS, s)
    ctx = k_ctx.shape[1]
    assert k_rows % v_lat.shape[3] == 0 and s % k_rows == 0
    q_spec = pl.BlockSpec((1, tq, group), lambda i, g, j: (i, j, g))
    k_spec = lambda t: pl.BlockSpec((1, t, hd), lambda i, g, j: (i, 0, g))
    v_spec = lambda v: pl.BlockSpec((1, v.shape[1], hd + ATTN_SUM_ROWS, v.shape[3]), lambda i, g, j: (i, 0, g, 0))
    specs = [q_spec, k_spec(s), v_spec(v_lat), k_spec(ctx), v_spec(v_ctx)]
    common = dict(
        grid=(b, A_KV_HEADS, s // tq),
        out_specs=q_spec,
        out_shape=jax.ShapeDtypeStruct((b, s, width), BF16),
        compiler_params=_params("parallel", "parallel", "arbitrary"),
    )

    def bounded(*args):
        return pl.pallas_call(functools.partial(_attn_bounded_kernel, k_rows=k_rows),
                              in_specs=[_resident((1, 1))] + specs, name="attn_bounded", **common)(*args)

    def online(_, *args):
        return pl.pallas_call(functools.partial(_attn_online_kernel, k_rows=k_rows),
                              in_specs=specs, name="attn_online", **common)(*args)

    safe = 2.0 * bound[0, 0] <= ATTN_SAFE_LOG2_SPAN
    return lax.cond(safe, bounded, online, bound, q, k_lat, v_lat, k_ctx, v_ctx)


def _tail_kernel(x_ref, hf_ref, hb_ref, og_ref, ya_ref, mg_ref, g1_ref, sc2_ref, sh2_ref, g2_ref,
                 gm_ref, n2_ref, fg_ref, w_pa, w_pb, w_o, w_g, w_u, w_d, o_ref):
    x = x_ref[0]
    d = x.shape[-1]
    hd = d // M_HEADS
    hsum = hf_ref[0].astype(F32) + hb_ref[0].astype(F32)
    hn = jnp.concatenate([_rms(hsum[:, i * hd:(i + 1) * hd], hd) for i in range(M_HEADS)], axis=-1)
    ym = (og_ref[0].astype(F32) * (hn * gm_ref[...])).astype(BF16)
    mg = mg_ref[0].astype(F32)
    merged = mg[:, :d] * _dot(ym, w_pa[...]) + mg[:, d:] * _dot(ya_ref[0], w_pb[...])
    x1 = x + g1_ref[0] * _dot(merged.astype(BF16), w_o[...])
    h2 = ((_rms(x1, d) * n2_ref[...]) * (1.0 + sc2_ref[0]) + sh2_ref[0]).astype(BF16)
    gate = _dot(h2, w_g[...])
    ff = _dot(((gate * _sigmoid(gate)) * _dot(h2, w_u[...])).astype(BF16), w_d[...])
    x2 = x1 + g2_ref[0] * ff
    o_ref[0] = _rms(x2, d) * fg_ref[...]


def _tail(x, hf, hb, og, ya, mg, g1, sc2, sh2, g2, gm, n2, fg, weights):
    b, t, d = x.shape
    tm = min(TAIL_ROWS, t)
    row = lambda n: pl.BlockSpec((1, tm, n), lambda i, j: (i, j, 0))
    per_b = pl.BlockSpec((1, 1, d), lambda i, j: (i, 0, 0))
    vec = _resident((1, d))
    return pl.pallas_call(
        _tail_kernel,
        grid=(b, t // tm),
        in_specs=[row(d), row(d), row(d), row(d), row(d), row(2 * d), per_b, per_b, per_b, per_b,
                  vec, vec, vec] + [_resident(w.shape) for w in weights],
        out_specs=row(d),
        out_shape=jax.ShapeDtypeStruct((b, t, d), F32),
        compiler_params=_params("parallel", "parallel"),
        name="tail",
    )(x, hf, hb, og, ya, mg, g1, sc2, sh2, g2, gm, n2, fg, *weights)


def _rope_tables(seq, head_dim):
    pairs = head_dim // 4
    pos = jnp.arange(seq, dtype=jnp.int32)
    row = (pos // GRID_W).astype(F32)
    col = (pos % GRID_W).astype(F32)
    inv = ROPE_THETA ** (-jnp.arange(pairs, dtype=F32) / pairs)
    ang = jnp.concatenate([row[:, None] * inv, col[:, None] * inv], axis=-1)
    cos, sin = jnp.cos(ang), jnp.sin(ang)
    return jnp.concatenate([cos, cos], axis=-1), jnp.concatenate([-sin, sin], axis=-1)


def _pair_split(head_dim):
    return jnp.concatenate([jnp.arange(0, head_dim, 2), jnp.arange(1, head_dim, 2)])


def kernel(x, c, ctx, c_ctx, w_mod, b_mod, norm1_g, norm2_g, w_in, gate_b, conv_w, conv_b, m_norm_g, q_norm_g,
           k_norm_g, w_pa, w_pb, w_o, w_ffn_gate, w_ffn_up, w_ffn_down, final_g):
    b, s, d = x.shape
    depth = w_mod.shape[0]
    assert depth == 1, "single trunk layer"
    m_hd = d // M_HEADS
    a_hd = d // A_HEADS
    kv_w = A_KV_HEADS * a_hd
    l = 0

    pad_rows = 8 * pl.cdiv(b + 1, 8) - (b + 1)
    c_rows = jnp.concatenate([c, c_ctx[None, :], jnp.zeros((pad_rows, d), F32)], axis=0)
    mod = _mod(c_rows, w_mod[l], b_mod[l])
    sh1, sc1, g1, sh2, sc2, g2 = [mod[:b, i * d:(i + 1) * d].reshape(b, 1, d) for i in range(6)]
    csh1, csc1 = [jnp.broadcast_to(mod[b, i * d:(i + 1) * d], (b, 1, d)) for i in range(2)]

    splits = (d, d, d, d, 4 * M_HEADS, d, kv_w, kv_w, 2 * d)
    offs = [0]
    for n in splits:
        offs.append(offs[-1] + n)
    cols = [w_in[l][:, offs[i]:offs[i + 1]] for i in range(len(splits))]
    perm = _pair_split(a_hd)
    perm_q = (jnp.arange(A_HEADS)[:, None] * a_hd + perm[None, :]).reshape(-1)
    perm_k = (jnp.arange(A_KV_HEADS)[:, None] * a_hd + perm[None, :]).reshape(-1)
    weights = [jnp.concatenate([cols[0], cols[1]], axis=1), cols[2], cols[3], cols[4],
               cols[5][:, perm_q], cols[6][:, perm_k], cols[7], cols[8]]
    weights = [w.astype(BF16) for w in weights]
    gq = (q_norm_g[l][perm] * (a_hd ** -0.5 * LOG2_E)).reshape(1, a_hd)
    gk = k_norm_g[l][perm].reshape(1, a_hd)
    score_bound = (a_hd * BF16_ROUNDING_MARGIN * jnp.max(jnp.abs(gq)) * jnp.max(jnp.abs(gk))).reshape(1, 1)
    gb = gate_b[l].reshape(-1, 1)
    cosf, sinf = _rope_tables(s, a_hd)
    n1 = norm1_g[l].reshape(1, d)

    proj = functools.partial(_inproj, gain=n1, cosf=cosf, sinf=sinf, gq=gq, gk=gk, gate_b=gb, conv_w=conv_w[l],
                             conv_b=conv_b[l], weights=weights, k_scale=m_hd ** -0.5)
    q_l, kt_l, v_l, og_l, gt_l, aq_l, ak_l, av_l, mg_l = proj(x, sc1, sh1, rope=True)
    q_c, kt_c, v_c, _, gt_c, _, ak_c, av_c, _ = proj(ctx, csc1, csh1, rope=False)

    chains = 2 * M_HEADS
    init = (jnp.zeros((b, chains, m_hd, m_hd), F32), jnp.zeros((b, chains, 8, m_hd), F32),
            jnp.full((b, chains, 1, 128), -jnp.inf, F32))
    ctx_state = _mlstm(q_c, kt_c, v_c, gt_c, init, emit_h=False)
    hf, hb = _mlstm(q_l, kt_l, v_l, gt_l, ctx_state, emit_h=True)[:2]

    ya = _attn(aq_l, ak_l, av_l, ak_c, av_c, score_bound)

    tail_w = [w.astype(BF16) for w in (w_pa[l], w_pb[l], w_o[l], w_ffn_gate[l], w_ffn_up[l], w_ffn_down[l])]
    return _tail(x, hf, hb, og_l, ya, mg_l, g1, sc2, sh2, g2, m_norm_g[l].reshape(1, d),
                 norm2_g[l].reshape(1, d), final_g.reshape(1, d), tail_w)
```

```python
import functools

import jax
import jax.numpy as jnp
from jax import lax
from jax.experimental import pallas as pl
from jax.experimental.pallas import tpu as pltpu

F32 = jnp.float32
BF16 = jnp.bfloat16

EPS = 1e-6
GRID_W = 64
ROPE_THETA = 10000.0
M_HEADS = 4
A_HEADS = 8
A_KV_HEADS = 2
A_GROUPS = A_HEADS // A_KV_HEADS
CONV_K = 5
CONV_PAD = CONV_K // 2
LOG2_E = 1.4426950408889634
BF16_ROUNDING_MARGIN = (1.0 + 2.0 ** -7) ** 2
ATTN_SAFE_LOG2_SPAN = 96.0

V7X_VMEM_BYTES = 64 * 1024 * 1024
VMEM_LIMIT = V7X_VMEM_BYTES - 8 * 1024 * 1024
F32_SUBLANES = 8
BF16_SUBLANES = 16
ATTN_SUM_ROWS = BF16_SUBLANES

INPROJ_ROWS = 256
ATTN_Q_ROWS = 512
ATTN_K_ROWS = 1024
TAIL_ROWS = 256


def _params(*sem):
    return pltpu.CompilerParams(dimension_semantics=sem, vmem_limit_bytes=VMEM_LIMIT)


def _resident(shape):
    zeros = (0,) * len(shape)
    return pl.BlockSpec(shape, lambda *_: zeros, pipeline_mode=pl.Buffered(1))


def _sigmoid(x):
    return 1.0 / (1.0 + jnp.exp(-x))


def _log_sigmoid(x):
    return jnp.minimum(x, 0.0) - jnp.log(1.0 + jnp.exp(-jnp.abs(x)))


def _rms(x, width):
    return x * lax.rsqrt(jnp.sum(x * x, axis=-1, keepdims=True) * (1.0 / width) + EPS)


def _dot(a, b):
    return jnp.dot(a, b, preferred_element_type=F32)


def _dot_nt(a, b):
    return lax.dot_general(a, b, (((1,), (1,)), ((), ())), preferred_element_type=F32)


def _dot_tn(a, b):
    return lax.dot_general(a, b, (((0,), (0,)), ((), ())), preferred_element_type=F32)


def _mod_kernel(c_ref, w_ref, b_ref, o_ref):
    c = c_ref[...]
    s = c * _sigmoid(c)
    o_ref[...] = jnp.dot(s, w_ref[...], preferred_element_type=F32, precision=lax.Precision.HIGHEST) + b_ref[...]


def _mod(c_rows, w_mod, b_mod):
    rows, d = c_rows.shape
    n = w_mod.shape[1]
    return pl.pallas_call(
        _mod_kernel,
        grid=(n // d,),
        in_specs=[
            pl.BlockSpec((rows, d), lambda j: (0, 0)),
            pl.BlockSpec((d, d), lambda j: (0, j)),
            pl.BlockSpec((1, d), lambda j: (0, j)),
        ],
        out_specs=pl.BlockSpec((rows, d), lambda j: (0, j)),
        out_shape=jax.ShapeDtypeStruct((rows, n), F32),
        compiler_params=_params("parallel"),
        name="mod",
    )(c_rows, w_mod, b_mod.reshape(1, n))


def _inproj_kernel(xp_ref, x_ref, xn_ref, sc_ref, sh_ref, g_ref, cos_ref, sin_ref, gq_ref, gk_ref, gb_ref,
                   cw_ref, cb_ref, w_mqk, w_gt, w_wide,
                   q_ref, kt_ref, v_ref, og_ref, gt_ref, aq_ref, ak_ref, av_ref, mg_ref, *, rope, k_scale):
    j = pl.program_id(1)
    last = pl.num_programs(1) - 1
    tm = x_ref.shape[1]
    halo = xp_ref.shape[1]
    d = x_ref.shape[-1]
    xs = jnp.concatenate([xp_ref[0], x_ref[0], xn_ref[0]], axis=0)
    h_ext = (_rms(xs, d) * g_ref[...]) * (1.0 + sc_ref[0]) + sh_ref[0]
    hb = h_ext[halo:halo + tm].astype(BF16)

    z = _dot(h_ext.astype(BF16), w_mqk[...])
    r = lax.broadcasted_iota(jnp.int32, (tm + 2 * halo, 1), 0)
    inside = ((r >= halo) | (j > 0)) & ((r < halo + tm) | (j < last))
    z = jnp.where(inside, z, 0.0)
    rows = tm + 2 * halo
    acc = cb_ref[...] + cw_ref[CONV_PAD:CONV_PAD + 1, :] * z[halo:halo + tm]
    for tap in range(CONV_K):
        if tap != CONV_PAD:
            shifted = pltpu.roll(z, (CONV_PAD - tap) % rows, axis=0)
            acc = acc + cw_ref[tap:tap + 1, :] * shifted[halo:halo + tm]
    y = acc * _sigmoid(acc)
    half = y.shape[-1] // 2
    q_ref[0] = y[:, :half].astype(BF16)
    kt_ref[0, 0] = (y[:, half:] * k_scale).astype(BF16).T

    gt = _dot_nt(w_gt[...], hb) + gb_ref[...]
    row = lax.broadcasted_iota(jnp.int32, gt.shape, 0)
    is_forget = (row // M_HEADS) % 2 == 1
    lf = jnp.where(is_forget, _log_sigmoid(gt), 0.0)
    half_rows = 2 * M_HEADS
    fwd, bwd = lf[:half_rows], lf[half_rows:]
    lane = lax.broadcasted_iota(jnp.int32, fwd.shape, 1)
    step = 1
    while step < tm:
        fwd = fwd + jnp.where(lane >= step, pltpu.roll(fwd, step, axis=1), 0.0)
        bwd = bwd + jnp.where(lane < tm - step, pltpu.roll(bwd, tm - step, axis=1), 0.0)
        step *= 2
    cum = jnp.concatenate([fwd, bwd], axis=0)
    cum_for_input = jnp.concatenate([cum[M_HEADS:], cum[:M_HEADS]], axis=0)
    gt_ref[0, 0] = jnp.where(is_forget, lf, gt - cum_for_input)

    def heads(z, gain, out_ref):
        hd = gain.shape[-1]
        for i in range(z.shape[-1] // hd):
            zh = _rms(z[:, i * hd:(i + 1) * hd], hd) * gain
            if rope:
                zh = zh * cos_ref[...] + pltpu.roll(zh, hd // 2, axis=1) * sin_ref[...]
            out_ref[0, :, i * hd:(i + 1) * hd] = zh.astype(BF16)

    wide = _dot(hb, w_wide[...])
    edges = [0]
    for ref in (aq_ref, ak_ref, v_ref, og_ref, mg_ref):
        edges.append(edges[-1] + ref.shape[-1])
    z_aq, z_ak, z_v, z_og, z_mg = [wide[:, lo:hi] for lo, hi in zip(edges[:-1], edges[1:])]
    heads(z_aq, gq_ref[...], aq_ref)
    heads(z_ak, gk_ref[...], ak_ref)
    v_ref[0] = z_v.astype(BF16)
    og_ref[0] = _sigmoid(z_og).astype(BF16)
    mg_ref[0] = _sigmoid(z_mg).astype(BF16)
    av = wide[:, edges[-1]:].astype(BF16)
    hd = gk_ref.shape[-1]
    vt_rows = hd + ATTN_SUM_ROWS
    sum_rows = jnp.where(lax.broadcasted_iota(jnp.int32, (ATTN_SUM_ROWS, tm), 0) == 0, 1.0, 0.0).astype(BF16)
    for i in range(av.shape[-1] // hd):
        av_ref[0, 0, i * vt_rows:i * vt_rows + hd, :] = av[:, i * hd:(i + 1) * hd].T
        av_ref[0, 0, i * vt_rows + hd:(i + 1) * vt_rows, :] = sum_rows


def _inproj(x, scale, shift, gain, cosf, sinf, gq, gk, gate_b, conv_w, conv_b, weights, *, rope, k_scale):
    b, t, d = x.shape
    tm = min(INPROJ_ROWS, t)
    hd = gq.shape[-1]
    widths = [w.shape[1] for w in weights]
    n_mqk, n_mv, n_mo, n_gt, n_aq, n_ak, n_av, n_mg = widths
    w_mqk, w_mv, w_mo, w_gt, w_aq, w_ak, w_av, w_mg = weights
    ws = [w_mqk, w_gt.T, jnp.concatenate([w_aq, w_ak, w_mv, w_mo, w_mg, w_av], axis=1)]
    n_avt = (n_av // hd) * (hd + ATTN_SUM_ROWS)
    halo = F32_SUBLANES
    per = tm // halo
    nblk = t // halo
    nt = t // tm
    row = lambda n: pl.BlockSpec((1, tm, n), lambda i, j: (i, j, 0))
    col = lambda n: pl.BlockSpec((1, 1, n, tm), lambda i, j: (i, j, 0, 0))
    prev_rows = pl.BlockSpec((1, halo, d), lambda i, j: (i, jnp.maximum(j * per - 1, 0), 0))
    next_rows = pl.BlockSpec((1, halo, d), lambda i, j: (i, jnp.minimum((j + 1) * per, nblk - 1), 0))
    per_b = pl.BlockSpec((1, 1, d), lambda i, j: (i, 0, 0))
    table = pl.BlockSpec((tm, hd), lambda i, j: (j, 0))
    out_shape = [
        jax.ShapeDtypeStruct((b, t, n_mqk // 2), BF16),
        jax.ShapeDtypeStruct((b, nt, n_mqk // 2, tm), BF16),
        jax.ShapeDtypeStruct((b, t, n_mv), BF16),
        jax.ShapeDtypeStruct((b, t, n_mo), BF16),
        jax.ShapeDtypeStruct((b, nt, n_gt, tm), F32),
        jax.ShapeDtypeStruct((b, t, n_aq), BF16),
        jax.ShapeDtypeStruct((b, t, n_ak), BF16),
        jax.ShapeDtypeStruct((b, nt, n_avt, tm), BF16),
        jax.ShapeDtypeStruct((b, t, n_mg), BF16),
    ]
    out_specs = [row(n_mqk // 2), col(n_mqk // 2), row(n_mv), row(n_mo), col(n_gt),
                 row(n_aq), row(n_ak), col(n_avt), row(n_mg)]
    return pl.pallas_call(
        functools.partial(_inproj_kernel, rope=rope, k_scale=k_scale),
        grid=(b, t // tm),
        in_specs=[prev_rows, row(d), next_rows, per_b, per_b, _resident((1, d)), table, table,
                  _resident((1, hd)), _resident((1, hd)), _resident((n_gt, 1)),
                  _resident(conv_w.shape), _resident((1, n_mqk))]
                 + [_resident(w.shape) for w in ws],
        out_specs=out_specs,
        out_shape=out_shape,
        compiler_params=_params("parallel", "parallel"),
        name="inproj_rope" if rope else "inproj_ctx",
    )(x, x, x, scale, shift, gain, cosf, sinf, gq, gk, gate_b, conv_w, conv_b.reshape(1, n_mqk), *ws)


def _mlstm_kernel(*refs, emit_h, has_init):
    qf_ref, kf_ref, vf_ref, gf_ref, qb_ref, kb_ref, vb_ref, gb_ref = refs[:8]
    refs = refs[8:]
    init_refs, refs = (refs[:3], refs[3:]) if has_init else (None, refs)
    (hf_ref, hb_ref), refs = (refs[:2], refs[2:]) if emit_h else ((None, None), refs)
    c_ref, n_ref, m_ref = refs

    @pl.when(pl.program_id(1) == 0)
    def _():
        if has_init:
            c_ref[...] = init_refs[0][...]
            n_ref[...] = init_refs[1][...]
            m_ref[...] = init_refs[2][...]
        else:
            c_ref[...] = jnp.zeros_like(c_ref)
            n_ref[...] = jnp.zeros_like(n_ref)
            m_ref[...] = jnp.full_like(m_ref, -jnp.inf)

    chunk = qf_ref.shape[1]
    hd = qf_ref.shape[2] // M_HEADS
    t_idx = lax.broadcasted_iota(jnp.int32, (chunk, chunk), 0)
    s_idx = lax.broadcasted_iota(jnp.int32, (chunk, chunk), 1)
    neg_inf = -jnp.inf
    ones = jnp.ones((chunk, 128), BF16)

    streams = ((qf_ref, kf_ref, vf_ref, gf_ref, hf_ref), (qb_ref, kb_ref, vb_ref, gb_ref, hb_ref))
    for direction, (q_ref, kt_ref, v_ref, g_ref, h_ref) in enumerate(streams):
        seen = s_idx <= t_idx if direction == 0 else s_idx >= t_idx
        gates = g_ref[0, 0]
        for head in range(M_HEADS):
            chain = head * 2 + direction
            r_b = (2 * direction) * M_HEADS + head
            r_f = (2 * direction + 1) * M_HEADS + head
            lf_row = gates[r_f:r_f + 1, :]
            b_row = gates[r_b:r_b + 1, :]
            f_total = jnp.sum(lf_row, axis=-1, keepdims=True)
            b_max = jnp.max(b_row, axis=-1, keepdims=True)

            lanes = slice(head * hd, (head + 1) * hd)
            q = q_ref[0, :, lanes]
            kt = kt_ref[0, 0, lanes, :]
            v = v_ref[0, :, lanes]
            c_state = c_ref[0, chain]
            n_state = n_ref[0, chain]
            m_state = m_ref[0, chain][:, :1]

            if emit_h:
                a_col = jnp.sum(jnp.where(seen, lf_row, 0.0), axis=-1, keepdims=True)
                cmb_col = jnp.max(jnp.where(seen, b_row, neg_inf), axis=-1, keepdims=True)
                u_col = jnp.maximum(m_state, cmb_col)
                p = jnp.exp(jnp.where(seen, b_row - u_col, neg_inf))
                sp = (_dot(q, kt) * p).astype(BF16)
                w_inter = jnp.exp(m_state - u_col)
                num = w_inter * _dot(q, c_state.astype(BF16)) + _dot(sp, v)
                qn = _dot_nt(q, jnp.broadcast_to(n_state[:1], (128, hd)).astype(BF16))
                den = w_inter * qn + _dot(sp, ones)
                inv = 1.0 / jnp.maximum(jnp.abs(den), jnp.exp(-(a_col + u_col)))
                h_ref[0, :, lanes] = (num * jnp.concatenate([inv] * (hd // 128), axis=-1)).astype(BF16)

            u_last = jnp.maximum(m_state, b_max)
            a_prev = jnp.exp(m_state - u_last)
            w_row = jnp.exp(b_row - u_last)
            ktw = kt * w_row.astype(BF16)
            c_ref[0, chain] = a_prev * c_state + _dot(ktw, v)
            n_ref[0, chain] = a_prev * n_state + _dot_nt(jnp.broadcast_to(w_row, (8, chunk)).astype(BF16), kt)
            m_ref[0, chain] = jnp.broadcast_to(f_total + u_last, m_ref.shape[2:])


def _mlstm(q, kt, v, gt, state, *, emit_h):
    b, t, width = q.shape
    nc, chunk = kt.shape[1], kt.shape[3]
    assert nc * chunk == t and gt.shape[1] == nc
    hd = width // M_HEADS
    chains = 2 * M_HEADS
    fwd = pl.BlockSpec((1, chunk, width), lambda i, j: (i, j, 0))
    bwd = pl.BlockSpec((1, chunk, width), lambda i, j: (i, nc - 1 - j, 0))
    col_fwd = lambda n: pl.BlockSpec((1, 1, n, chunk), lambda i, j: (i, j, 0, 0))
    col_bwd = lambda n: pl.BlockSpec((1, 1, n, chunk), lambda i, j: (i, nc - 1 - j, 0, 0))
    c_spec = pl.BlockSpec((1, chains, hd, hd), lambda i, j: (i, 0, 0, 0))
    n_spec = pl.BlockSpec((1, chains, 8, hd), lambda i, j: (i, 0, 0, 0))
    m_spec = pl.BlockSpec((1, chains, 1, 128), lambda i, j: (i, 0, 0, 0))
    state_shapes = [jax.ShapeDtypeStruct((b, chains, hd, hd), F32),
                    jax.ShapeDtypeStruct((b, chains, 8, hd), F32),
                    jax.ShapeDtypeStruct((b, chains, 1, 128), F32)]
    h_shapes = [jax.ShapeDtypeStruct((b, t, width), BF16)] * 2 if emit_h else []
    h_specs = [fwd, bwd] if emit_h else []
    n_g = gt.shape[2]
    state_specs = [c_spec, n_spec, m_spec]
    return pl.pallas_call(
        functools.partial(_mlstm_kernel, emit_h=emit_h, has_init=state is not None),
        grid=(b, nc),
        in_specs=[fwd, col_fwd(width), fwd, col_fwd(n_g), bwd, col_bwd(width), bwd, col_bwd(n_g)]
                 + (state_specs if state is not None else []),
        out_specs=h_specs + state_specs,
        out_shape=h_shapes + state_shapes,
        compiler_params=_params("parallel", "arbitrary"),
        name="mlstm" if emit_h else "mlstm_ctx",
    )(q, kt, v, gt, q, kt, v, gt, *(state or ()))


def _stack_heads(q, hd):
    return jnp.concatenate([q[:, i * hd:(i + 1) * hd] for i in range(A_GROUPS)], axis=0)


def _store_heads(o_ref, acc_t, tq, hd):
    out = (acc_t[:hd] / acc_t[hd:hd + 1]).T
    for i in range(A_GROUPS):
        o_ref[0, :, i * hd:(i + 1) * hd] = out[i * tq:(i + 1) * tq, :].astype(BF16)


def _key_chunks(kl_ref, vl_ref, kc_ref, vc_ref, k_rows):
    def vt(ref, first, count):
        return jnp.concatenate([ref[0, first + i] for i in range(count)], axis=-1)

    tile = vl_ref.shape[3]
    yield kc_ref[0], vt(vc_ref, 0, vc_ref.shape[1])
    for start in range(0, kl_ref.shape[1], k_rows):
        yield kl_ref[0, start:start + k_rows, :], vt(vl_ref, start // tile, k_rows // tile)


def _attn_bounded_kernel(bound_ref, q_ref, kl_ref, vl_ref, kc_ref, vc_ref, o_ref, *, k_rows):
    tq = q_ref.shape[1]
    hd = kl_ref.shape[2]
    qs = _stack_heads(q_ref[0], hd)
    bound = bound_ref[...]
    acc = None
    for k, vt in _key_chunks(kl_ref, vl_ref, kc_ref, vc_ref, k_rows):
        part = _dot(vt, jnp.exp2(_dot_nt(k, qs) - bound).astype(BF16))
        acc = part if acc is None else acc + part
    _store_heads(o_ref, acc, tq, hd)


def _attn_online_kernel(q_ref, kl_ref, vl_ref, kc_ref, vc_ref, o_ref, *, k_rows):
    tq = q_ref.shape[1]
    hd = kl_ref.shape[2]
    qs = _stack_heads(q_ref[0], hd)
    m = jnp.full((1, qs.shape[0]), -jnp.inf, F32)
    acc = jnp.zeros((vc_ref.shape[2], qs.shape[0]), F32)
    for k, vt in _key_chunks(kl_ref, vl_ref, kc_ref, vc_ref, k_rows):
        s = _dot_nt(k, qs)
        m_new = jnp.maximum(m, jnp.max(s, axis=0, keepdims=True))
        acc = jnp.exp2(m - m_new) * acc + _dot(vt, jnp.exp2(s - m_new).astype(BF16))
        m = m_new
    _store_heads(o_ref, acc, tq, hd)


def _attn(q, k_lat, v_lat, k_ctx, v_ctx, bound):
    b, s, width = q.shape
    hd = width // A_HEADS
    group = A_GROUPS * hd
    tq = min(ATTN_Q_ROWS, s)
    k_rows = min(ATTN_K_ROWS, s)
    ctx = k_ctx.shape[1]
    assert k_rows % v_lat.shape[3] == 0 and s % k_rows == 0
    q_spec = pl.BlockSpec((1, tq, group), lambda i, g, j: (i, j, g))
    k_spec = lambda t: pl.BlockSpec((1, t, hd), lambda i, g, j: (i, 0, g))
    v_spec = lambda v: pl.BlockSpec((1, v.shape[1], hd + ATTN_SUM_ROWS, v.shape[3]), lambda i, g, j: (i, 0, g, 0))
    specs = [q_spec, k_spec(s), v_spec(v_lat), k_spec(ctx), v_spec(v_ctx)]
    common = dict(
        grid=(b, A_KV_HEADS, s // tq),
        out_specs=q_spec,
        out_shape=jax.ShapeDtypeStruct((b, s, width), BF16),
        compiler_params=_params("parallel", "parallel", "arbitrary"),
    )

    def bounded(*args):
        return pl.pallas_call(functools.partial(_attn_bounded_kernel, k_rows=k_rows),
                              in_specs=[_resident((1, 1))] + specs, name="attn_bounded", **common)(*args)

    def online(_, *args):
        return pl.pallas_call(functools.partial(_attn_online_kernel, k_rows=k_rows),
                              in_specs=specs, name="attn_online", **common)(*args)

    safe = 2.0 * bound[0, 0] <= ATTN_SAFE_LOG2_SPAN
    return lax.cond(safe, bounded, online, bound, q, k_lat, v_lat, k_ctx, v_ctx)


def _tail_kernel(x_ref, hf_ref, hb_ref, og_ref, ya_ref, mg_ref, g1_ref, sc2_ref, sh2_ref, g2_ref,
                 gm_ref, n2_ref, fg_ref, w_pa, w_pb, w_o, w_g, w_u, w_d, o_ref):
    x = x_ref[0]
    d = x.shape[-1]
    hd = d // M_HEADS
    attn = _dot(ya_ref[0], w_pb[...])
    hsum = hf_ref[0].astype(F32) + hb_ref[0].astype(F32)
    hn = jnp.concatenate([_rms(hsum[:, i * hd:(i + 1) * hd], hd) for i in range(M_HEADS)], axis=-1)
    ym = (og_ref[0].astype(F32) * (hn * gm_ref[...])).astype(BF16)
    mg = mg_ref[0].astype(F32)
    merged = mg[:, d:] * attn + mg[:, :d] * _dot(ym, w_pa[...])
    x1 = x + g1_ref[0] * _dot(merged.astype(BF16), w_o[...])
    h2 = ((_rms(x1, d) * n2_ref[...]) * (1.0 + sc2_ref[0]) + sh2_ref[0]).astype(BF16)
    gate = _dot(h2, w_g[...])
    ff = _dot(((gate * _sigmoid(gate)) * _dot(h2, w_u[...])).astype(BF16), w_d[...])
    x2 = x1 + g2_ref[0] * ff
    o_ref[0] = _rms(x2, d) * fg_ref[...]


def _tail(x, hf, hb, og, ya, mg, g1, sc2, sh2, g2, gm, n2, fg, weights):
    b, t, d = x.shape
    tm = min(TAIL_ROWS, t)
    row = lambda n: pl.BlockSpec((1, tm, n), lambda i, j: (i, j, 0))
    per_b = pl.BlockSpec((1, 1, d), lambda i, j: (i, 0, 0))
    vec = _resident((1, d))
    return pl.pallas_call(
        _tail_kernel,
        grid=(b, t // tm),
        in_specs=[row(d), row(d), row(d), row(d), row(d), row(2 * d), per_b, per_b, per_b, per_b,
                  vec, vec, vec] + [_resident(w.shape) for w in weights],
        out_specs=row(d),
        out_shape=jax.ShapeDtypeStruct((b, t, d), F32),
        compiler_params=_params("parallel", "parallel"),
        name="tail",
    )(x, hf, hb, og, ya, mg, g1, sc2, sh2, g2, gm, n2, fg, *weights)


def _rope_tables(seq, head_dim):
    pairs = head_dim // 4
    pos = jnp.arange(seq, dtype=jnp.int32)
    row = (pos // GRID_W).astype(F32)
    col = (pos % GRID_W).astype(F32)
    inv = ROPE_THETA ** (-jnp.arange(pairs, dtype=F32) / pairs)
    ang = jnp.concatenate([row[:, None] * inv, col[:, None] * inv], axis=-1)
    cos, sin = jnp.cos(ang), jnp.sin(ang)
    return jnp.concatenate([cos, cos], axis=-1), jnp.concatenate([-sin, sin], axis=-1)


def _pair_split(head_dim):
    return jnp.concatenate([jnp.arange(0, head_dim, 2), jnp.arange(1, head_dim, 2)])


def kernel(x, c, ctx, c_ctx, w_mod, b_mod, norm1_g, norm2_g, w_in, gate_b, conv_w, conv_b, m_norm_g, q_norm_g,
           k_norm_g, w_pa, w_pb, w_o, w_ffn_gate, w_ffn_up, w_ffn_down, final_g):
    b, s, d = x.shape
    depth = w_mod.shape[0]
    assert depth == 1, "single trunk layer"
    m_hd = d // M_HEADS
    a_hd = d // A_HEADS
    kv_w = A_KV_HEADS * a_hd
    l = 0

    pad_rows = 8 * pl.cdiv(b + 1, 8) - (b + 1)
    c_rows = jnp.concatenate([c, c_ctx[None, :], jnp.zeros((pad_rows, d), F32)], axis=0)
    mod = _mod(c_rows, w_mod[l], b_mod[l])
    sh1, sc1, g1, sh2, sc2, g2 = [mod[:b, i * d:(i + 1) * d].reshape(b, 1, d) for i in range(6)]
    csh1, csc1 = [jnp.broadcast_to(mod[b, i * d:(i + 1) * d], (b, 1, d)) for i in range(2)]

    splits = (d, d, d, d, 4 * M_HEADS, d, kv_w, kv_w, 2 * d)
    offs = [0]
    for n in splits:
        offs.append(offs[-1] + n)
    cols = [w_in[l][:, offs[i]:offs[i + 1]] for i in range(len(splits))]
    perm = _pair_split(a_hd)
    perm_q = (jnp.arange(A_HEADS)[:, None] * a_hd + perm[None, :]).reshape(-1)
    perm_k = (jnp.arange(A_KV_HEADS)[:, None] * a_hd + perm[None, :]).reshape(-1)
    weights = [jnp.concatenate([cols[0], cols[1]], axis=1), cols[2], cols[3], cols[4],
               cols[5][:, perm_q], cols[6][:, perm_k], cols[7], cols[8]]
    weights = [w.astype(BF16) for w in weights]
    gq = (q_norm_g[l][perm] * (a_hd ** -0.5 * LOG2_E)).reshape(1, a_hd)
    gk = k_norm_g[l][perm].reshape(1, a_hd)
    score_bound = (a_hd * BF16_ROUNDING_MARGIN * jnp.max(jnp.abs(gq)) * jnp.max(jnp.abs(gk))).reshape(1, 1)
    gb = gate_b[l].reshape(-1, 1)
    cosf, sinf = _rope_tables(s, a_hd)
    n1 = norm1_g[l].reshape(1, d)

    proj = functools.partial(_inproj, gain=n1, cosf=cosf, sinf=sinf, gq=gq, gk=gk, gate_b=gb, conv_w=conv_w[l],
                             conv_b=conv_b[l], weights=weights, k_scale=m_hd ** -0.5)
    q_l, kt_l, v_l, og_l, gt_l, aq_l, ak_l, av_l, mg_l = proj(x, sc1, sh1, rope=True)
    q_c, kt_c, v_c, _, gt_c, _, ak_c, av_c, _ = proj(ctx, csc1, csh1, rope=False)

    ctx_state = _mlstm(q_c, kt_c, v_c, gt_c, None, emit_h=False)
    hf, hb = _mlstm(q_l, kt_l, v_l, gt_l, ctx_state, emit_h=True)[:2]

    ya = _attn(aq_l, ak_l, av_l, ak_c, av_c, score_bound)

    tail_w = [w.astype(BF16) for w in (w_pa[l], w_pb[l], w_o[l], w_ffn_gate[l], w_ffn_up[l], w_ffn_down[l])]
    return _tail(x, hf, hb, og_l, ya, mg_l, g1, sc2, sh2, g2, m_norm_g[l].reshape(1, d),
                 norm2_g[l].reshape(1, d), final_g.reshape(1, d), tail_w)
```

```python
import functools

import jax
import jax.numpy as jnp
from jax import lax
from jax.experimental import pallas as pl
from jax.experimental.pallas import tpu as pltpu

F32 = jnp.float32
BF16 = jnp.bfloat16

EPS = 1e-6
GRID_W = 64
ROPE_THETA = 10000.0
M_HEADS = 4
A_HEADS = 8
A_KV_HEADS = 2
A_GROUPS = A_HEADS // A_KV_HEADS
CONV_K = 5
CONV_PAD = CONV_K // 2
LOG2_E = 1.4426950408889634
BF16_ROUNDING_MARGIN = (1.0 + 2.0 ** -7) ** 2
ATTN_SAFE_LOG2_SPAN = 96.0

V7X_VMEM_BYTES = 64 * 1024 * 1024
VMEM_LIMIT = V7X_VMEM_BYTES - 8 * 1024 * 1024
F32_SUBLANES = 8

INPROJ_ROWS = 256
ATTN_Q_ROWS = 512
ATTN_K_ROWS = 1024
TAIL_ROWS = 512


def _params(*sem):
    return pltpu.CompilerParams(dimension_semantics=sem, vmem_limit_bytes=VMEM_LIMIT)


def _resident(shape):
    zeros = (0,) * len(shape)
    return pl.BlockSpec(shape, lambda *_: zeros, pipeline_mode=pl.Buffered(1))


def _sigmoid(x):
    return 1.0 / (1.0 + jnp.exp(-x))


def _log_sigmoid(x):
    return jnp.minimum(x, 0.0) - jnp.log(1.0 + jnp.exp(-jnp.abs(x)))


def _rms(x, width):
    return x * lax.rsqrt(jnp.sum(x * x, axis=-1, keepdims=True) * (1.0 / width) + EPS)


def _dot(a, b):
    return jnp.dot(a, b, preferred_element_type=F32)


def _dot_nt(a, b):
    return lax.dot_general(a, b, (((1,), (1,)), ((), ())), preferred_element_type=F32)


def _dot_tn(a, b):
    return lax.dot_general(a, b, (((0,), (0,)), ((), ())), preferred_element_type=F32)


def _mod_kernel(c_ref, w_ref, b_ref, o_ref):
    c = c_ref[...]
    s = c * _sigmoid(c)
    o_ref[...] = jnp.dot(s, w_ref[...], preferred_element_type=F32, precision=lax.Precision.HIGHEST) + b_ref[...]


def _mod(c_rows, w_mod, b_mod):
    rows, d = c_rows.shape
    n = w_mod.shape[1]
    return pl.pallas_call(
        _mod_kernel,
        grid=(n // d,),
        in_specs=[
            pl.BlockSpec((rows, d), lambda j: (0, 0)),
            pl.BlockSpec((d, d), lambda j: (0, j)),
            pl.BlockSpec((1, d), lambda j: (0, j)),
        ],
        out_specs=pl.BlockSpec((rows, d), lambda j: (0, j)),
        out_shape=jax.ShapeDtypeStruct((rows, n), F32),
        compiler_params=_params("parallel"),
        name="mod",
    )(c_rows, w_mod, b_mod.reshape(1, n))


def _inproj_kernel(xp_ref, x_ref, xn_ref, sc_ref, sh_ref, g_ref, cos_ref, sin_ref, gq_ref, gk_ref, gb_ref,
                   cw_ref, cb_ref, w_mqk, w_gt, w_wide,
                   q_ref, kt_ref, v_ref, og_ref, gt_ref, aq_ref, ak_ref, av_ref, mg_ref, *, rope, k_scale):
    j = pl.program_id(1)
    last = pl.num_programs(1) - 1
    tm = x_ref.shape[1]
    halo = xp_ref.shape[1]
    d = x_ref.shape[-1]
    xs = jnp.concatenate([xp_ref[0], x_ref[0], xn_ref[0]], axis=0)
    h_ext = (_rms(xs, d) * g_ref[...]) * (1.0 + sc_ref[0]) + sh_ref[0]
    hb = h_ext[halo:halo + tm].astype(BF16)

    z = _dot(h_ext.astype(BF16), w_mqk[...])
    r = lax.broadcasted_iota(jnp.int32, (tm + 2 * halo, 1), 0)
    inside = ((r >= halo) | (j > 0)) & ((r < halo + tm) | (j < last))
    z = jnp.where(inside, z, 0.0)
    rows = tm + 2 * halo
    acc = cb_ref[...] + cw_ref[CONV_PAD:CONV_PAD + 1, :] * z[halo:halo + tm]
    for tap in range(CONV_K):
        if tap != CONV_PAD:
            shifted = pltpu.roll(z, (CONV_PAD - tap) % rows, axis=0)
            acc = acc + cw_ref[tap:tap + 1, :] * shifted[halo:halo + tm]
    y = acc * _sigmoid(acc)
    half = y.shape[-1] // 2
    q_ref[0] = y[:, :half].astype(BF16)
    kt_ref[0, 0] = (y[:, half:] * k_scale).astype(BF16).T

    gt = _dot_nt(w_gt[...], hb) + gb_ref[...]
    row = lax.broadcasted_iota(jnp.int32, gt.shape, 0)
    is_forget = (row // M_HEADS) % 2 == 1
    lf = jnp.where(is_forget, _log_sigmoid(gt), 0.0)
    half_rows = 2 * M_HEADS
    fwd, bwd = lf[:half_rows], lf[half_rows:]
    lane = lax.broadcasted_iota(jnp.int32, fwd.shape, 1)
    step = 1
    while step < tm:
        fwd = fwd + jnp.where(lane >= step, pltpu.roll(fwd, step, axis=1), 0.0)
        bwd = bwd + jnp.where(lane < tm - step, pltpu.roll(bwd, tm - step, axis=1), 0.0)
        step *= 2
    cum = jnp.concatenate([fwd, bwd], axis=0)
    cum_for_input = jnp.concatenate([cum[M_HEADS:], cum[:M_HEADS]], axis=0)
    gt_ref[0, 0] = jnp.where(is_forget, lf, gt - cum_for_input)

    def heads(z, gain, out_ref):
        hd = gain.shape[-1]
        for i in range(z.shape[-1] // hd):
            zh = _rms(z[:, i * hd:(i + 1) * hd], hd) * gain
            if rope:
                zh = zh * cos_ref[...] + pltpu.roll(zh, hd // 2, axis=1) * sin_ref[...]
            out_ref[0, :, i * hd:(i + 1) * hd] = zh.astype(BF16)

    wide = _dot(hb, w_wide[...])
    edges = [0]
    for ref in (aq_ref, ak_ref, v_ref, og_ref, mg_ref):
        edges.append(edges[-1] + ref.shape[-1])
    z_aq, z_ak, z_v, z_og, z_mg = [wide[:, lo:hi] for lo, hi in zip(edges[:-1], edges[1:])]
    heads(z_aq, gq_ref[...], aq_ref)
    heads(z_ak, gk_ref[...], ak_ref)
    v_ref[0] = z_v.astype(BF16)
    og_ref[0] = _sigmoid(z_og).astype(BF16)
    mg_ref[0] = _sigmoid(z_mg).astype(BF16)
    av_ref[0, 0] = wide[:, edges[-1]:].astype(BF16).T


def _inproj(x, scale, shift, gain, cosf, sinf, gq, gk, gate_b, conv_w, conv_b, weights, *, rope, k_scale):
    b, t, d = x.shape
    tm = min(INPROJ_ROWS, t)
    hd = gq.shape[-1]
    widths = [w.shape[1] for w in weights]
    n_mqk, n_mv, n_mo, n_gt, n_aq, n_ak, n_av, n_mg = widths
    w_mqk, w_mv, w_mo, w_gt, w_aq, w_ak, w_av, w_mg = weights
    ws = [w_mqk, w_gt.T, jnp.concatenate([w_aq, w_ak, w_mv, w_mo, w_mg, w_av], axis=1)]
    halo = F32_SUBLANES
    per = tm // halo
    nblk = t // halo
    nt = t // tm
    row = lambda n: pl.BlockSpec((1, tm, n), lambda i, j: (i, j, 0))
    col = lambda n: pl.BlockSpec((1, 1, n, tm), lambda i, j: (i, j, 0, 0))
    prev_rows = pl.BlockSpec((1, halo, d), lambda i, j: (i, jnp.maximum(j * per - 1, 0), 0))
    next_rows = pl.BlockSpec((1, halo, d), lambda i, j: (i, jnp.minimum((j + 1) * per, nblk - 1), 0))
    per_b = pl.BlockSpec((1, 1, d), lambda i, j: (i, 0, 0))
    table = pl.BlockSpec((tm, hd), lambda i, j: (j, 0))
    out_shape = [
        jax.ShapeDtypeStruct((b, t, n_mqk // 2), BF16),
        jax.ShapeDtypeStruct((b, nt, n_mqk // 2, tm), BF16),
        jax.ShapeDtypeStruct((b, t, n_mv), BF16),
        jax.ShapeDtypeStruct((b, t, n_mo), BF16),
        jax.ShapeDtypeStruct((b, nt, n_gt, tm), F32),
        jax.ShapeDtypeStruct((b, t, n_aq), BF16),
        jax.ShapeDtypeStruct((b, t, n_ak), BF16),
        jax.ShapeDtypeStruct((b, nt, n_av, tm), BF16),
        jax.ShapeDtypeStruct((b, t, n_mg), BF16),
    ]
    out_specs = [row(n_mqk // 2), col(n_mqk // 2), row(n_mv), row(n_mo), col(n_gt),
                 row(n_aq), row(n_ak), col(n_av), row(n_mg)]
    return pl.pallas_call(
        functools.partial(_inproj_kernel, rope=rope, k_scale=k_scale),
        grid=(b, t // tm),
        in_specs=[prev_rows, row(d), next_rows, per_b, per_b, _resident((1, d)), table, table,
                  _resident((1, hd)), _resident((1, hd)), _resident((n_gt, 1)),
                  _resident(conv_w.shape), _resident((1, n_mqk))]
                 + [_resident(w.shape) for w in ws],
        out_specs=out_specs,
        out_shape=out_shape,
        compiler_params=_params("parallel", "parallel"),
        name="inproj_rope" if rope else "inproj_ctx",
    )(x, x, x, scale, shift, gain, cosf, sinf, gq, gk, gate_b, conv_w, conv_b.reshape(1, n_mqk), *ws)


def _mlstm_kernel(*refs, emit_h, has_init):
    qf_ref, kf_ref, vf_ref, gf_ref, qb_ref, kb_ref, vb_ref, gb_ref = refs[:8]
    refs = refs[8:]
    init_refs, refs = (refs[:3], refs[3:]) if has_init else (None, refs)
    (hf_ref, hb_ref), refs = (refs[:2], refs[2:]) if emit_h else ((None, None), refs)
    c_ref, n_ref, m_ref = refs

    @pl.when(pl.program_id(1) == 0)
    def _():
        if has_init:
            c_ref[...] = init_refs[0][...]
            n_ref[...] = init_refs[1][...]
            m_ref[...] = init_refs[2][...]
        else:
            c_ref[...] = jnp.zeros_like(c_ref)
            n_ref[...] = jnp.zeros_like(n_ref)
            m_ref[...] = jnp.full_like(m_ref, -jnp.inf)

    chunk = qf_ref.shape[1]
    hd = qf_ref.shape[2] // M_HEADS
    t_idx = lax.broadcasted_iota(jnp.int32, (chunk, chunk), 0)
    s_idx = lax.broadcasted_iota(jnp.int32, (chunk, chunk), 1)
    neg_inf = -jnp.inf
    ones = jnp.ones((chunk, 128), BF16)

    streams = ((qf_ref, kf_ref, vf_ref, gf_ref, hf_ref), (qb_ref, kb_ref, vb_ref, gb_ref, hb_ref))
    for direction, (q_ref, kt_ref, v_ref, g_ref, h_ref) in enumerate(streams):
        seen = s_idx <= t_idx if direction == 0 else s_idx >= t_idx
        gates = g_ref[0, 0]
        for head in range(M_HEADS):
            chain = head * 2 + direction
            r_b = (2 * direction) * M_HEADS + head
            r_f = (2 * direction + 1) * M_HEADS + head
            lf_row = gates[r_f:r_f + 1, :]
            b_row = gates[r_b:r_b + 1, :]
            f_total = jnp.sum(lf_row, axis=-1, keepdims=True)
            b_max = jnp.max(b_row, axis=-1, keepdims=True)

            lanes = slice(head * hd, (head + 1) * hd)
            q = q_ref[0, :, lanes]
            kt = kt_ref[0, 0, lanes, :]
            v = v_ref[0, :, lanes]
            c_state = c_ref[0, chain]
            n_state = n_ref[0, chain]
            m_state = m_ref[0, chain][:, :1]

            if emit_h:
                a_col = jnp.sum(jnp.where(seen, lf_row, 0.0), axis=-1, keepdims=True)
                cmb_col = jnp.max(jnp.where(seen, b_row, neg_inf), axis=-1, keepdims=True)
                u_col = jnp.maximum(m_state, cmb_col)
                p = jnp.exp(jnp.where(seen, b_row - u_col, neg_inf))
                sp = (_dot(q, kt) * p).astype(BF16)
                w_inter = jnp.exp(m_state - u_col)
                num = w_inter * _dot(q, c_state.astype(BF16)) + _dot(sp, v)
                qn = _dot_nt(q, jnp.broadcast_to(n_state[:1], (128, hd)).astype(BF16))
                den = w_inter * qn + _dot(sp, ones)
                inv = 1.0 / jnp.maximum(jnp.abs(den), jnp.exp(-(a_col + u_col)))
                h_ref[0, :, lanes] = (num * jnp.concatenate([inv] * (hd // 128), axis=-1)).astype(BF16)

            u_last = jnp.maximum(m_state, b_max)
            a_prev = jnp.exp(m_state - u_last)
            w_row = jnp.exp(b_row - u_last)
            ktw = kt * w_row.astype(BF16)
            c_ref[0, chain] = a_prev * c_state + _dot(ktw, v)
            n_ref[0, chain] = a_prev * n_state + _dot_nt(jnp.broadcast_to(w_row, (8, chunk)).astype(BF16), kt)
            m_ref[0, chain] = jnp.broadcast_to(f_total + u_last, m_ref.shape[2:])


def _mlstm(q, kt, v, gt, state, *, emit_h):
    b, t, width = q.shape
    nc, chunk = kt.shape[1], kt.shape[3]
    assert nc * chunk == t and gt.shape[1] == nc
    hd = width // M_HEADS
    chains = 2 * M_HEADS
    fwd = pl.BlockSpec((1, chunk, width), lambda i, j: (i, j, 0))
    bwd = pl.BlockSpec((1, chunk, width), lambda i, j: (i, nc - 1 - j, 0))
    col_fwd = lambda n: pl.BlockSpec((1, 1, n, chunk), lambda i, j: (i, j, 0, 0))
    col_bwd = lambda n: pl.BlockSpec((1, 1, n, chunk), lambda i, j: (i, nc - 1 - j, 0, 0))
    c_spec = pl.BlockSpec((1, chains, hd, hd), lambda i, j: (i, 0, 0, 0))
    n_spec = pl.BlockSpec((1, chains, 8, hd), lambda i, j: (i, 0, 0, 0))
    m_spec = pl.BlockSpec((1, chains, 1, 128), lambda i, j: (i, 0, 0, 0))
    state_shapes = [jax.ShapeDtypeStruct((b, chains, hd, hd), F32),
                    jax.ShapeDtypeStruct((b, chains, 8, hd), F32),
                    jax.ShapeDtypeStruct((b, chains, 1, 128), F32)]
    h_shapes = [jax.ShapeDtypeStruct((b, t, width), BF16)] * 2 if emit_h else []
    h_specs = [fwd, bwd] if emit_h else []
    n_g = gt.shape[2]
    state_specs = [c_spec, n_spec, m_spec]
    return pl.pallas_call(
        functools.partial(_mlstm_kernel, emit_h=emit_h, has_init=state is not None),
        grid=(b, nc),
        in_specs=[fwd, col_fwd(width), fwd, col_fwd(n_g), bwd, col_bwd(width), bwd, col_bwd(n_g)]
                 + (state_specs if state is not None else []),
        out_specs=h_specs + state_specs,
        out_shape=h_shapes + state_shapes,
        compiler_params=_params("parallel", "arbitrary"),
        name="mlstm" if emit_h else "mlstm_ctx",
    )(q, kt, v, gt, q, kt, v, gt, *(state or ()))


def _stack_heads(q, hd):
    return jnp.concatenate([q[:, i * hd:(i + 1) * hd] for i in range(A_GROUPS)], axis=0)


def _store_heads(o_ref, num_t, den, tq, hd):
    out = (num_t / den).T
    for i in range(A_GROUPS):
        o_ref[0, :, i * hd:(i + 1) * hd] = out[i * tq:(i + 1) * tq, :].astype(BF16)


def _key_chunks(kl_ref, vl_ref, kc_ref, vc_ref, k_rows):
    def vt(ref, first, count):
        return jnp.concatenate([ref[0, first + i] for i in range(count)], axis=-1)

    tile = vl_ref.shape[3]
    yield kc_ref[0], vt(vc_ref, 0, vc_ref.shape[1])
    for start in range(0, kl_ref.shape[1], k_rows):
        yield kl_ref[0, start:start + k_rows, :], vt(vl_ref, start // tile, k_rows // tile)


def _attn_bounded_kernel(bound_ref, q_ref, kl_ref, vl_ref, kc_ref, vc_ref, o_ref, *, k_rows):
    tq = q_ref.shape[1]
    hd = kl_ref.shape[2]
    qs = _stack_heads(q_ref[0], hd)
    bound = bound_ref[...]
    num = den = None
    for k, vt in _key_chunks(kl_ref, vl_ref, kc_ref, vc_ref, k_rows):
        p = jnp.exp2(_dot_nt(k, qs) - bound)
        num_part = _dot(vt, p.astype(BF16))
        den_part = jnp.sum(p, axis=0, keepdims=True)
        num, den = (num_part, den_part) if num is None else (num + num_part, den + den_part)
    _store_heads(o_ref, num, den, tq, hd)


def _attn_online_kernel(q_ref, kl_ref, vl_ref, kc_ref, vc_ref, o_ref, *, k_rows):
    tq = q_ref.shape[1]
    hd = kl_ref.shape[2]
    qs = _stack_heads(q_ref[0], hd)
    m = jnp.full((1, qs.shape[0]), -jnp.inf, F32)
    den = jnp.zeros((1, qs.shape[0]), F32)
    num = jnp.zeros((hd, qs.shape[0]), F32)
    for k, vt in _key_chunks(kl_ref, vl_ref, kc_ref, vc_ref, k_rows):
        s = _dot_nt(k, qs)
        m_new = jnp.maximum(m, jnp.max(s, axis=0, keepdims=True))
        alpha = jnp.exp2(m - m_new)
        p = jnp.exp2(s - m_new)
        num = alpha * num + _dot(vt, p.astype(BF16))
        den = alpha * den + jnp.sum(p, axis=0, keepdims=True)
        m = m_new
    _store_heads(o_ref, num, den, tq, hd)


def _attn(q, k_lat, v_lat, k_ctx, v_ctx, bound):
    b, s, width = q.shape
    hd = width // A_HEADS
    group = A_GROUPS * hd
    tq = min(ATTN_Q_ROWS, s)
    k_rows = min(ATTN_K_ROWS, s)
    ctx = k_ctx.shape[1]
    assert k_rows % v_lat.shape[3] == 0 and s % k_rows == 0
    q_spec = pl.BlockSpec((1, tq, group), lambda i, g, j: (i, j, g))
    k_spec = lambda t: pl.BlockSpec((1, t, hd), lambda i, g, j: (i, 0, g))
    v_spec = lambda v: pl.BlockSpec((1, v.shape[1], hd, v.shape[3]), lambda i, g, j: (i, 0, g, 0))
    specs = [q_spec, k_spec(s), v_spec(v_lat), k_spec(ctx), v_spec(v_ctx)]
    common = dict(
        grid=(b, A_KV_HEADS, s // tq),
        out_specs=q_spec,
        out_shape=jax.ShapeDtypeStruct((b, s, width), BF16),
        compiler_params=_params("parallel", "parallel", "arbitrary"),
    )

    def bounded(*args):
        return pl.pallas_call(functools.partial(_attn_bounded_kernel, k_rows=k_rows),
                              in_specs=[_resident((1, 1))] + specs, name="attn_bounded", **common)(*args)

    def online(_, *args):
        return pl.pallas_call(functools.partial(_attn_online_kernel, k_rows=k_rows),
                              in_specs=specs, name="attn_online", **common)(*args)

    safe = 2.0 * bound[0, 0] <= ATTN_SAFE_LOG2_SPAN
    return lax.cond(safe, bounded, online, bound, q, k_lat, v_lat, k_ctx, v_ctx)


def _tail_kernel(x_ref, hf_ref, hb_ref, og_ref, ya_ref, mg_ref, g1_ref, sc2_ref, sh2_ref, g2_ref,
                 gm_ref, n2_ref, fg_ref, w_pa, w_pb, w_o, w_g, w_u, w_d, o_ref):
    x = x_ref[0]
    d = x.shape[-1]
    hd = d // M_HEADS
    attn = _dot(ya_ref[0], w_pb[...])
    hsum = hf_ref[0].astype(F32) + hb_ref[0].astype(F32)
    hn = jnp.concatenate([_rms(hsum[:, i * hd:(i + 1) * hd], hd) for i in range(M_HEADS)], axis=-1)
    ym = (og_ref[0].astype(F32) * (hn * gm_ref[...])).astype(BF16)
    mg = mg_ref[0].astype(F32)
    merged = mg[:, d:] * attn + mg[:, :d] * _dot(ym, w_pa[...])
    x1 = x + g1_ref[0] * _dot(merged.astype(BF16), w_o[...])
    h2 = ((_rms(x1, d) * n2_ref[...]) * (1.0 + sc2_ref[0]) + sh2_ref[0]).astype(BF16)
    gate = _dot(h2, w_g[...])
    ff = _dot(((gate * _sigmoid(gate)) * _dot(h2, w_u[...])).astype(BF16), w_d[...])
    x2 = x1 + g2_ref[0] * ff
    o_ref[0] = _rms(x2, d) * fg_ref[...]


def _tail(x, hf, hb, og, ya, mg, g1, sc2, sh2, g2, gm, n2, fg, weights):
    b, t, d = x.shape
    tm = min(TAIL_ROWS, t)
    row = lambda n: pl.BlockSpec((1, tm, n), lambda i, j: (i, j, 0))
    per_b = pl.BlockSpec((1, 1, d), lambda i, j: (i, 0, 0))
    vec = _resident((1, d))
    return pl.pallas_call(
        _tail_kernel,
        grid=(b, t // tm),
        in_specs=[row(d), row(d), row(d), row(d), row(d), row(2 * d), per_b, per_b, per_b, per_b,
                  vec, vec, vec] + [_resident(w.shape) for w in weights],
        out_specs=row(d),
        out_shape=jax.ShapeDtypeStruct((b, t, d), F32),
        compiler_params=_params("parallel", "parallel"),
        name="tail",
    )(x, hf, hb, og, ya, mg, g1, sc2, sh2, g2, gm, n2, fg, *weights)


def _rope_tables(seq, head_dim):
    pairs = head_dim // 4
    rows = seq // GRID_W
    inv = ROPE_THETA ** (-jnp.arange(pairs, dtype=F32) / pairs)
    row_ang = jnp.arange(rows, dtype=F32)[:, None] * inv
    col_ang = jnp.arange(GRID_W, dtype=F32)[:, None] * inv

    def table(fn):
        by_row = jnp.broadcast_to(fn(row_ang)[:, None, :], (rows, GRID_W, pairs))
        by_col = jnp.broadcast_to(fn(col_ang)[None, :, :], (rows, GRID_W, pairs))
        return jnp.concatenate([by_row, by_col], axis=-1).reshape(seq, 2 * pairs)

    cos, sin = table(jnp.cos), table(jnp.sin)
    return jnp.concatenate([cos, cos], axis=-1), jnp.concatenate([-sin, sin], axis=-1)


def _pair_split(head_dim):
    return jnp.concatenate([jnp.arange(0, head_dim, 2), jnp.arange(1, head_dim, 2)])


def kernel(x, c, ctx, c_ctx, w_mod, b_mod, norm1_g, norm2_g, w_in, gate_b, conv_w, conv_b, m_norm_g, q_norm_g,
           k_norm_g, w_pa, w_pb, w_o, w_ffn_gate, w_ffn_up, w_ffn_down, final_g):
    b, s, d = x.shape
    depth = w_mod.shape[0]
    assert depth == 1, "single trunk layer"
    m_hd = d // M_HEADS
    a_hd = d // A_HEADS
    kv_w = A_KV_HEADS * a_hd
    l = 0

    pad_rows = 8 * pl.cdiv(b + 1, 8) - (b + 1)
    c_rows = jnp.concatenate([c, c_ctx[None, :], jnp.zeros((pad_rows, d), F32)], axis=0)
    mod = _mod(c_rows, w_mod[l], b_mod[l])
    sh1, sc1, g1, sh2, sc2, g2 = [mod[:b, i * d:(i + 1) * d].reshape(b, 1, d) for i in range(6)]
    csh1, csc1 = [jnp.broadcast_to(mod[b, i * d:(i + 1) * d], (b, 1, d)) for i in range(2)]

    splits = (d, d, d, d, 4 * M_HEADS, d, kv_w, kv_w, 2 * d)
    offs = [0]
    for n in splits:
        offs.append(offs[-1] + n)
    cols = [w_in[l][:, offs[i]:offs[i + 1]] for i in range(len(splits))]
    perm = _pair_split(a_hd)
    perm_q = (jnp.arange(A_HEADS)[:, None] * a_hd + perm[None, :]).reshape(-1)
    perm_k = (jnp.arange(A_KV_HEADS)[:, None] * a_hd + perm[None, :]).reshape(-1)
    weights = [jnp.concatenate([cols[0], cols[1]], axis=1), cols[2], cols[3], cols[4],
               cols[5][:, perm_q], cols[6][:, perm_k], cols[7], cols[8]]
    weights = [w.astype(BF16) for w in weights]
    gq = (q_norm_g[l][perm] * (a_hd ** -0.5 * LOG2_E)).reshape(1, a_hd)
    gk = k_norm_g[l][perm].reshape(1, a_hd)
    score_bound = (a_hd * BF16_ROUNDING_MARGIN * jnp.max(jnp.abs(gq)) * jnp.max(jnp.abs(gk))).reshape(1, 1)
    gb = gate_b[l].reshape(-1, 1)
    cosf, sinf = _rope_tables(s, a_hd)
    n1 = norm1_g[l].reshape(1, d)

    proj = functools.partial(_inproj, gain=n1, cosf=cosf, sinf=sinf, gq=gq, gk=gk, gate_b=gb, conv_w=conv_w[l],
                             conv_b=conv_b[l], weights=weights, k_scale=m_hd ** -0.5)
    q_l, kt_l, v_l, og_l, gt_l, aq_l, ak_l, av_l, mg_l = proj(x, sc1, sh1, rope=True)
    q_c, kt_c, v_c, _, gt_c, _, ak_c, av_c, _ = proj(ctx, csc1, csh1, rope=False)

    ctx_state = _mlstm(q_c, kt_c, v_c, gt_c, None, emit_h=False)
    hf, hb = _mlstm(q_l, kt_l, v_l, gt_l, ctx_state, emit_h=True)[:2]

    ya = _attn(aq_l, ak_l, av_l, ak_c, av_c, score_bound)

    tail_w = [w.astype(BF16) for w in (w_pa[l], w_pb[l], w_o[l], w_ffn_gate[l], w_ffn_up[l], w_ffn_down[l])]
    return _tail(x, hf, hb, og_l, ya, mg_l, g1, sc2, sh2, g2, m_norm_g[l].reshape(1, d),
                 norm2_g[l].reshape(1, d), final_g.reshape(1, d), tail_w)
```
